```python
import math
import jax
import jax.numpy as jnp
from jax import lax
import numpy as np

D_MODEL = 1024
BATCH = 8
SEQ = 2048
DEPTH = 2

CHUNK = 64
N_EVEN = (DEPTH + 1) // 2
N_ODD = DEPTH // 2
NORM_EPS = 1e-6

S5_WIDTH = D_MODEL // 2
S5_GROUP = 16
S5_GROUPS = S5_WIDTH // S5_GROUP
S5_STATE = 64

SSD_INNER = D_MODEL
SSD_HEADDIM = 64
SSD_HEADS = SSD_INNER // SSD_HEADDIM
SSD_STATE = 64
SSD_GROUPS = 4
SSD_CONV = 4
SSD_CONV_DIM = SSD_INNER + 2 * SSD_GROUPS * SSD_STATE
EVEN_IN = S5_WIDTH + SSD_INNER + SSD_CONV_DIM + SSD_HEADS
EVEN_MIX = S5_WIDTH + SSD_INNER

RET_HEADS = 4
RET_QK = D_MODEL // 8
RET_V = 2 * RET_QK
RET_KEY_W = RET_HEADS * RET_QK
RET_VAL_W = RET_HEADS * RET_V
ROPE_BASE = 10000.0

GLA_HEADS = 4
GLA_QK = D_MODEL // 8
GLA_V = D_MODEL // 4
GLA_KEY_W = GLA_HEADS * GLA_QK
GLA_VAL_W = GLA_HEADS * GLA_V
GLA_RANK = 16
GLA_TAU = 16.0
ODD_IN = 2 * RET_KEY_W + 2 * RET_VAL_W + 2 * GLA_KEY_W + 2 * GLA_VAL_W + GLA_RANK
ODD_MIX = RET_VAL_W + GLA_VAL_W

FFN_HIDDEN = -(-8 * D_MODEL // (3 * 256)) * 256

kernel_name = "hybrid_s5_ssd_retention_gla_trunk"


def rmsnorm(x, g):
    xf = x.astype(jnp.float32)
    y = xf * lax.rsqrt(jnp.mean(xf * xf, axis=-1, keepdims=True) + NORM_EPS)
    return (y * g.astype(jnp.float32)).astype(x.dtype)


def split_cols(t, sizes):
    parts, start = [], 0
    for s in sizes:
        parts.append(t[..., start:start + s])
        start += s
    return parts


def to_chunks(t):
    bsz, seqlen, nh, d = t.shape
    return t.reshape(bsz, seqlen // CHUNK, CHUNK, nh, d).transpose(1, 0, 3, 2, 4)


def from_chunks(t):
    nc, bsz, nh, c, d = t.shape
    return t.transpose(1, 0, 3, 2, 4).reshape(bsz, nc * c, nh, d)


def causal_depthwise_conv(x, w, b):
    width, ch = w.shape
    y = lax.conv_general_dilated(x, w[:, None, :], window_strides=(1,),
                                 padding=[(width - 1, 0)],
                                 dimension_numbers=('NWC', 'WIO', 'NWC'),
                                 feature_group_count=ch)
    return y + b


def rotary(t):
    seqlen, d = t.shape[1], t.shape[-1]
    half = d // 2
    inv_freq = ROPE_BASE ** (-jnp.arange(half, dtype=jnp.float32) / half)
    ang = jnp.arange(seqlen, dtype=jnp.float32)[:, None] * inv_freq[None, :]
    cos, sin = jnp.cos(ang)[None, :, None, :], jnp.sin(ang)[None, :, None, :]
    t1, t2 = t[..., :half], t[..., half:]
    return jnp.concatenate([t1 * cos - t2 * sin, t1 * sin + t2 * cos], axis=-1)


def s5_combine(left, right):
    al_re, al_im, bl_re, bl_im = left
    ar_re, ar_im, br_re, br_im = right
    return (al_re * ar_re - al_im * ar_im,
            al_re * ar_im + al_im * ar_re,
            ar_re * bl_re - ar_im * bl_im + br_re,
            ar_re * bl_im + ar_im * bl_re + br_im)


def s5_mixer(u, lam_re, lam_im, log_step, b_re, b_im, c_re, c_im, d_skip, glu_w, glu_b):
    f32 = jnp.float32
    bsz, seqlen, _ = u.shape
    ug = u.astype(f32).reshape(bsz, seqlen, S5_GROUPS, S5_GROUP)
    lr, li = lam_re.astype(f32), lam_im.astype(f32)
    step = jnp.exp(log_step.astype(f32))[:, None]
    mag = jnp.exp(lr * step)
    a_re, a_im = mag * jnp.cos(li * step), mag * jnp.sin(li * step)
    den = lr * lr + li * li
    k_re = ((a_re - 1.0) * lr + a_im * li) / den
    k_im = (a_im * lr - (a_re - 1.0) * li) / den
    br, bi = b_re.astype(f32), b_im.astype(f32)
    bb_re = k_re[..., None] * br - k_im[..., None] * bi
    bb_im = k_re[..., None] * bi + k_im[..., None] * br
    bu_re = jnp.einsum('gpn,blgn->lbgp', bb_re, ug)
    bu_im = jnp.einsum('gpn,blgn->lbgp', bb_im, ug)
    shape_a = (seqlen, 1, S5_GROUPS, S5_STATE)
    a_re_t = jnp.broadcast_to(a_re, shape_a)
    a_im_t = jnp.broadcast_to(a_im, shape_a)
    _, _, xs_re, xs_im = lax.associative_scan(s5_combine, (a_re_t, a_im_t, bu_re, bu_im), axis=0)
    y = (jnp.einsum('gnp,lbgp->blgn', c_re.astype(f32), xs_re)
         - jnp.einsum('gnp,lbgp->blgn', c_im.astype(f32), xs_im)
         + d_skip.astype(f32) * ug).reshape(bsz, seqlen, S5_WIDTH)
    z = jax.nn.gelu(y)
    out = z * jax.nn.sigmoid(z @ glu_w.astype(f32) + glu_b.astype(f32))
    return out.astype(u.dtype)


def ssd_mixer(z, xbc, dt_raw, conv_w, conv_b, dt_bias, a_log, d_skip, norm_g):
    f32 = jnp.float32
    bsz, seqlen, _ = z.shape
    nc = seqlen // CHUNK
    xbc = jax.nn.silu(causal_depthwise_conv(xbc.astype(f32), conv_w.astype(f32), conv_b.astype(f32)))
    xs, bm, cm = split_cols(xbc, (SSD_INNER, SSD_GROUPS * SSD_STATE, SSD_GROUPS * SSD_STATE))
    hpg = SSD_HEADS // SSD_GROUPS
    x = xs.reshape(bsz, nc, CHUNK, SSD_HEADS, SSD_HEADDIM)
    bm = jnp.repeat(bm.reshape(bsz, nc, CHUNK, SSD_GROUPS, SSD_STATE), hpg, axis=3)
    cm = jnp.repeat(cm.reshape(bsz, nc, CHUNK, SSD_GROUPS, SSD_STATE), hpg, axis=3)
    dt = jax.nn.softplus(dt_raw.astype(f32) + dt_bias.astype(f32))
    da = (dt * -jnp.exp(a_log.astype(f32))).reshape(bsz, nc, CHUNK, SSD_HEADS).transpose(0, 3, 1, 2)
    xdt = x * dt.reshape(bsz, nc, CHUNK, SSD_HEADS)[..., None]
    a_cum = jnp.cumsum(da, axis=-1)
    causal = jnp.tril(jnp.ones((CHUNK, CHUNK), dtype=bool))
    seg = jnp.exp(jnp.where(causal, a_cum[..., :, None] - a_cum[..., None, :], -jnp.inf))
    scores = jnp.einsum('bclhn,bcshn->bhcls', cm, bm) * seg
    y = jnp.einsum('bhcls,bcshp->bclhp', scores, xdt)
    decay_to_end = jnp.exp(a_cum[..., -1:] - a_cum)
    states = jnp.einsum('bclhn,bhcl,bclhp->cbhpn', bm, decay_to_end, xdt)
    chunk_decay = jnp.exp(a_cum[..., -1]).transpose(2, 0, 1)

    def step(carry, inp):
        st, dec = inp
        return carry * dec[..., None, None] + st, carry

    init = jnp.zeros((bsz, SSD_HEADS, SSD_HEADDIM, SSD_STATE), f32)
    _, prev = lax.scan(step, init, (states, chunk_decay))
    y = y + jnp.einsum('bclhn,cbhpn,bhcl->bclhp', cm, prev, jnp.exp(a_cum))
    y = y + d_skip.astype(f32)[:, None] * x
    y = y.reshape(bsz, seqlen, SSD_INNER) * jax.nn.silu(z.astype(f32))
    return rmsnorm(y, norm_g).astype(z.dtype)


def retention_mixer(q, k, v, g, norm_g):
    f32 = jnp.float32
    bsz, seqlen, _ = q.shape
    qh = rotary(q.astype(f32).reshape(bsz, seqlen, RET_HEADS, RET_QK))
    kh = rotary(k.astype(f32).reshape(bsz, seqlen, RET_HEADS, RET_QK)) * RET_QK ** -0.5
    vh = v.astype(f32).reshape(bsz, seqlen, RET_HEADS, RET_V)
    log_gamma = jnp.log(1.0 - 2.0 ** (-5.0 - jnp.arange(RET_HEADS, dtype=f32)))
    pos = jnp.arange(CHUNK, dtype=f32)
    diff = pos[:, None] - pos[None, :]
    dmat = jnp.where(diff >= 0, jnp.exp(log_gamma[:, None, None] * jnp.maximum(diff, 0.0)), 0.0)
    q_decay = jnp.exp(log_gamma[:, None] * (pos + 1.0))[..., None]
    k_decay = jnp.exp(log_gamma[:, None] * (CHUNK - 1.0 - pos))
    chunk_decay = jnp.exp(log_gamma * CHUNK)[:, None, None]

    def step(state, inp):
        qc, kc, vc = inp
        inner = jnp.einsum('bhid,bhjd->bhij', qc, kc) * dmat
        out = (jnp.einsum('bhij,bhjv->bhiv', inner, vc)
               + jnp.einsum('bhid,bhdv->bhiv', qc, state) * q_decay)
        state = state * chunk_decay + jnp.einsum('bhjd,hj,bhjv->bhdv', kc, k_decay, vc)
        return state, out

    init = jnp.zeros((bsz, RET_HEADS, RET_QK, RET_V), f32)
    _, out = lax.scan(step, init, (to_chunks(qh), to_chunks(kh), to_chunks(vh)))
    o = from_chunks(out)
    mu = jnp.mean(o, axis=-1, keepdims=True)
    var = jnp.mean(jnp.square(o - mu), axis=-1, keepdims=True)
    o = ((o - mu) * lax.rsqrt(var + NORM_EPS)).reshape(bsz, seqlen, RET_VAL_W) * norm_g.astype(f32)
    return (o * jax.nn.silu(g.astype(f32))).astype(q.dtype)


def gla_mixer(q, k, v, a_lr, g, gate_w, gate_b, norm_g):
    f32 = jnp.float32
    bsz, seqlen, _ = q.shape
    log_a = jax.nn.log_sigmoid(a_lr.astype(f32) @ gate_w.astype(f32) + gate_b.astype(f32)) / GLA_TAU
    qh = q.astype(f32).reshape(bsz, seqlen, GLA_HEADS, GLA_QK) * GLA_QK ** -0.5
    kh = k.astype(f32).reshape(bsz, seqlen, GLA_HEADS, GLA_QK)
    vh = v.astype(f32).reshape(bsz, seqlen, GLA_HEADS, GLA_V)
    ah = log_a.reshape(bsz, seqlen, GLA_HEADS, GLA_QK)
    causal = jnp.tril(jnp.ones((CHUNK, CHUNK), dtype=bool))

    def step(state, inp):
        qc, kc, vc, ac = inp
        bcum = jnp.cumsum(ac, axis=2)
        blast = bcum[:, :, -1:, :]
        q_in = qc * jnp.exp(bcum)
        att = jnp.where(causal, jnp.einsum('bhid,bhjd->bhij', q_in, kc * jnp.exp(-bcum)), 0.0)
        out = (jnp.einsum('bhij,bhjv->bhiv', att, vc)
               + jnp.einsum('bhid,bhdv->bhiv', q_in, state))
        state = (state * jnp.exp(blast)[:, :, 0, :, None]
                 + jnp.einsum('bhjd,bhjv->bhdv', kc * jnp.exp(blast - bcum), vc))
        return state, out

    init = jnp.zeros((bsz, GLA_HEADS, GLA_QK, GLA_V), f32)
    _, out = lax.scan(step, init, (to_chunks(qh), to_chunks(kh), to_chunks(vh), to_chunks(ah)))
    o = from_chunks(out)
    o = (o * lax.rsqrt(jnp.mean(o * o, axis=-1, keepdims=True) + NORM_EPS)).reshape(bsz, seqlen, GLA_VAL_W)
    o = o * norm_g.astype(f32)
    return (o * jax.nn.silu(g.astype(f32))).astype(q.dtype)


def swiglu(h, w_gate, w_up, w_down):
    return (jax.nn.silu(h @ w_gate) * (h @ w_up)) @ w_down


def setup_inputs(seed: int = 0) -> dict:
    key = jax.random.key(seed)
    ks = iter(jax.random.split(key, 48))
    f32 = jnp.float32

    def normal(shape, scale):
        return jax.random.normal(next(ks), shape, f32) * scale

    def gain(shape):
        return 1.0 + normal(shape, 0.02)

    def loguniform(shape, lo, hi):
        return jax.random.uniform(next(ks), shape, f32, math.log(lo), math.log(hi))

    x = normal((BATCH, SEQ, D_MODEL), 1.0)
    norm_mix_g = gain((DEPTH, D_MODEL))
    norm_ffn_g = gain((DEPTH, D_MODEL))
    final_norm_g = gain((D_MODEL,))

    ev_in_w = normal((N_EVEN, D_MODEL, EVEN_IN), D_MODEL ** -0.5)
    s5_lam_re = -0.5 + normal((N_EVEN, S5_GROUPS, S5_STATE), 0.01)
    s5_lam_im = (math.pi * jnp.arange(S5_STATE, dtype=f32)) + normal((N_EVEN, S5_GROUPS, S5_STATE), 0.01)
    s5_log_step = loguniform((N_EVEN, S5_GROUPS), 1e-3, 1e-1)
    b_scale = (2.0 * S5_GROUP) ** -0.5
    s5_b_re = normal((N_EVEN, S5_GROUPS, S5_STATE, S5_GROUP), b_scale)
    s5_b_im = normal((N_EVEN, S5_GROUPS, S5_STATE, S5_GROUP), b_scale)
    c_scale = (2.0 * S5_STATE) ** -0.5
    s5_c_re = normal((N_EVEN, S5_GROUPS, S5_GROUP, S5_STATE), c_scale)
    s5_c_im = normal((N_EVEN, S5_GROUPS, S5_GROUP, S5_STATE), c_scale)
    s5_d = normal((N_EVEN, S5_GROUPS, S5_GROUP), 1.0)
    s5_glu_w = normal((N_EVEN, S5_WIDTH, S5_WIDTH), S5_WIDTH ** -0.5)
    s5_glu_b = normal((N_EVEN, S5_WIDTH), 0.02)
    ssd_conv_w = normal((N_EVEN, SSD_CONV, SSD_CONV_DIM), SSD_CONV ** -0.5)
    ssd_conv_b = normal((N_EVEN, SSD_CONV_DIM), 0.02)
    dt0 = jnp.exp(loguniform((N_EVEN, SSD_HEADS), 1e-3, 1e-1))
    ssd_dt_bias = dt0 + jnp.log(-jnp.expm1(-dt0))
    ssd_a_log = jnp.log(jax.random.uniform(next(ks), (N_EVEN, SSD_HEADS), f32, 1.0, 16.0))
    ssd_d = gain((N_EVEN, SSD_HEADS))
    ssd_norm_g = gain((N_EVEN, SSD_INNER))
    ev_out_w = normal((N_EVEN, EVEN_MIX, D_MODEL), EVEN_MIX ** -0.5)

    od_in_w = normal((N_ODD, D_MODEL, ODD_IN), D_MODEL ** -0.5)
    ret_norm_g = gain((N_ODD, RET_VAL_W))
    gla_gate_w = normal((N_ODD, GLA_RANK, GLA_KEY_W), GLA_RANK ** -0.5)
    gla_gate_b = normal((N_ODD, GLA_KEY_W), 0.1)
    gla_norm_g = gain((N_ODD, GLA_VAL_W))
    od_out_w = normal((N_ODD, ODD_MIX, D_MODEL), ODD_MIX ** -0.5)

    ffn_gate_w = normal((DEPTH, D_MODEL, FFN_HIDDEN), D_MODEL ** -0.5)
    ffn_up_w = normal((DEPTH, D_MODEL, FFN_HIDDEN), D_MODEL ** -0.5)
    ffn_down_w = normal((DEPTH, FFN_HIDDEN, D_MODEL), FFN_HIDDEN ** -0.5)

    return {
        "x": x, "norm_mix_g": norm_mix_g, "norm_ffn_g": norm_ffn_g, "final_norm_g": final_norm_g,
        "ev_in_w": ev_in_w, "s5_lam_re": s5_lam_re, "s5_lam_im": s5_lam_im, "s5_log_step": s5_log_step,
        "s5_b_re": s5_b_re, "s5_b_im": s5_b_im, "s5_c_re": s5_c_re, "s5_c_im": s5_c_im, "s5_d": s5_d,
        "s5_glu_w": s5_glu_w, "s5_glu_b": s5_glu_b, "ssd_conv_w": ssd_conv_w, "ssd_conv_b": ssd_conv_b,
        "ssd_dt_bias": ssd_dt_bias, "ssd_a_log": ssd_a_log, "ssd_d": ssd_d, "ssd_norm_g": ssd_norm_g,
        "ev_out_w": ev_out_w, "od_in_w": od_in_w, "ret_norm_g": ret_norm_g, "gla_gate_w": gla_gate_w,
        "gla_gate_b": gla_gate_b, "gla_norm_g": gla_norm_g, "od_out_w": od_out_w,
        "ffn_gate_w": ffn_gate_w, "ffn_up_w": ffn_up_w, "ffn_down_w": ffn_down_w,
    }


def reference(x, norm_mix_g, norm_ffn_g, final_norm_g, ev_in_w, s5_lam_re, s5_lam_im, s5_log_step,
              s5_b_re, s5_b_im, s5_c_re, s5_c_im, s5_d, s5_glu_w, s5_glu_b, ssd_conv_w, ssd_conv_b,
              ssd_dt_bias, ssd_a_log, ssd_d, ssd_norm_g, ev_out_w, od_in_w, ret_norm_g, gla_gate_w,
              gla_gate_b, gla_norm_g, od_out_w, ffn_gate_w, ffn_up_w, ffn_down_w):
    h = x
    for layer in range(DEPTH):
        i = layer // 2
        hn = rmsnorm(h, norm_mix_g[layer])
        if layer % 2 == 0:
            proj = hn @ ev_in_w[i]
            u, z, xbc, dt_raw = split_cols(proj, (S5_WIDTH, SSD_INNER, SSD_CONV_DIM, SSD_HEADS))
            y_a = s5_mixer(u, s5_lam_re[i], s5_lam_im[i], s5_log_step[i], s5_b_re[i], s5_b_im[i],
                           s5_c_re[i], s5_c_im[i], s5_d[i], s5_glu_w[i], s5_glu_b[i])
            y_b = ssd_mixer(z, xbc, dt_raw, ssd_conv_w[i], ssd_conv_b[i], ssd_dt_bias[i],
                            ssd_a_log[i], ssd_d[i], ssd_norm_g[i])
            mix = jnp.concatenate([y_a, y_b], axis=-1) @ ev_out_w[i]
        else:
            proj = hn @ od_in_w[i]
            q_r, k_r, v_r, g_r, q_g, k_g, v_g, a_lr, g_g = split_cols(
                proj, (RET_KEY_W, RET_KEY_W, RET_VAL_W, RET_VAL_W,
                       GLA_KEY_W, GLA_KEY_W, GLA_VAL_W, GLA_RANK, GLA_VAL_W))
            y_c = retention_mixer(q_r, k_r, v_r, g_r, ret_norm_g[i])
            y_d = gla_mixer(q_g, k_g, v_g, a_lr, g_g, gla_gate_w[i], gla_gate_b[i], gla_norm_g[i])
            mix = jnp.concatenate([y_c, y_d], axis=-1) @ od_out_w[i]
        h = h + mix
        hn = rmsnorm(h, norm_ffn_g[layer])
        h = h + swiglu(hn, ffn_gate_w[layer], ffn_up_w[layer], ffn_down_w[layer])
    return rmsnorm(h, final_norm_g)
```

```python
import functools
import math

import jax
import jax.numpy as jnp
from jax import lax
from jax.experimental import pallas as pl
from jax.experimental.pallas import tpu as pltpu

F32 = jnp.float32
BF16 = jnp.bfloat16

NORM_EPS = 1e-6
VMEM_LIMIT_BYTES = 56 * 1024 * 1024

ROW_TILE = 512
FFN_CHUNK = 256


def _rms(x, g):
    return x * lax.rsqrt(jnp.mean(x * x, axis=-1, keepdims=True) + NORM_EPS) * g


def _dot(a, b):
    return jnp.dot(a, b, preferred_element_type=F32)


def _resident(shape):
    zeros = (0,) * len(shape)
    return pl.BlockSpec(shape, lambda *_: zeros, pipeline_mode=pl.Buffered(1))


def _params(*semantics):
    return pltpu.CompilerParams(dimension_semantics=semantics,
                                vmem_limit_bytes=VMEM_LIMIT_BYTES)


def _norm_proj_kernel(x_ref, g_ref, *refs):
    n_out = len(refs) // 2
    xn = _rms(x_ref[...], g_ref[...]).astype(BF16)
    for w_ref, o_ref in zip(refs[:n_out], refs[n_out:]):
        o_ref[...] = _dot(xn, w_ref[...]).astype(o_ref.dtype)


def norm_proj(x, g, weights, out_dtypes, row_tile=ROW_TILE):
    rows, d = x.shape
    assert rows % row_tile == 0
    in_specs = [pl.BlockSpec((row_tile, d), lambda i: (i, 0)), _resident((1, d))]
    in_specs += [_resident(w.shape) for w in weights]
    out_specs = [pl.BlockSpec((row_tile, w.shape[1]), lambda i: (i, 0)) for w in weights]
    out_shape = [jax.ShapeDtypeStruct((rows, w.shape[1]), dt) for w, dt in zip(weights, out_dtypes)]
    return pl.pallas_call(
        _norm_proj_kernel, grid=(rows // row_tile,), in_specs=in_specs, out_specs=out_specs,
        out_shape=out_shape, compiler_params=_params("parallel"), name="norm_proj",
    )(x, g.reshape(1, d), *weights)


def _mix_ffn_kernel(h_ref, ya_ref, yb_ref, wa_ref, wb_ref, g_ref, wg_ref, wu_ref, wd_ref,
                    fg_ref, o_ref, act_ref, *, final_norm):
    h1 = h_ref[...] + _dot(ya_ref[...], wa_ref[...]) + _dot(yb_ref[...], wb_ref[...])
    xn = _rms(h1, g_ref[...]).astype(BF16)
    hidden = wg_ref.shape[1]
    for j in range(hidden // FFN_CHUNK):
        cols = slice(j * FFN_CHUNK, (j + 1) * FFN_CHUNK)
        gate = _dot(xn, wg_ref[:, cols])
        up = _dot(xn, wu_ref[:, cols])
        act_ref[:, cols] = (gate * jax.nn.sigmoid(gate) * up).astype(BF16)
    out = h1 + _dot(act_ref[...], wd_ref[...])
    if final_norm:
        out = _rms(out, fg_ref[...])
    o_ref[...] = out


def mix_ffn(h, ya, yb, wa, wb, g, wg, wu, wd, fg, final_norm, row_tile=ROW_TILE):
    rows, d = h.shape
    hidden = wg.shape[1]
    assert rows % row_tile == 0 and hidden % FFN_CHUNK == 0
    row = lambda n: pl.BlockSpec((row_tile, n), lambda i: (i, 0))
    return pl.pallas_call(
        functools.partial(_mix_ffn_kernel, final_norm=final_norm),
        grid=(rows // row_tile,),
        in_specs=[row(d), row(ya.shape[1]), row(yb.shape[1]), _resident(wa.shape), _resident(wb.shape),
                  _resident((1, d)), _resident(wg.shape), _resident(wu.shape), _resident(wd.shape),
                  _resident((1, d))],
        out_specs=row(d),
        out_shape=jax.ShapeDtypeStruct((rows, d), F32),
        scratch_shapes=[pltpu.VMEM((row_tile, hidden), BF16)],
        compiler_params=_params("parallel"), name="mix_ffn",
    )(h, ya, yb, wa, wb, g.reshape(1, d), wg, wu, wd, fg.reshape(1, d))


def _split3(a):
    hi = a.astype(BF16)
    r1 = a - hi.astype(F32)
    mid = r1.astype(BF16)
    lo = (r1 - mid.astype(F32)).astype(BF16)
    return hi, mid, lo


def _select_dot(a, sel):
    hi, mid, lo = _split3(a)
    return _dot(hi, sel) + _dot(mid, sel) + _dot(lo, sel)


def _dot_select(sel, a):
    hi, mid, lo = _split3(a)
    return _dot(sel, hi) + _dot(sel, mid) + _dot(sel, lo)


def _dot_nt(a, b):
    return lax.dot_general(a, b, (((1,), (1,)), ((), ())), preferred_element_type=F32)


def _silu(x):
    return x * jax.nn.sigmoid(x)


def _softplus(x):
    return jnp.maximum(x, 0.0) + jnp.log1p(jnp.exp(-jnp.abs(x)))


SSD_TILE = 256
SSD_HEADS = 16
SSD_HEADDIM = 64
SSD_GROUPS = 4
SSD_STATE = 64
SSD_CONV = 4
LANES = 128


def _ssd_kernel(z_ref, xbc_ref, dt_ref, cw_ref, cb_ref, dtb_ref, alog_ref, dsk_ref, ng_ref,
                expand_ref, tril_ref, hmask_ref, gmask_ref, bdmask_ref,
                o_ref, xpad_ref, state_ref, lhs_ref, rhs_ref):
    cs = z_ref.shape[0]
    inner = z_ref.shape[1]
    gw = inner // SSD_GROUPS
    hpg = SSD_HEADS // SSD_GROUPS

    @pl.when(pl.program_id(1) == 0)
    def _():
        xpad_ref[0:8, :] = jnp.zeros((8, xpad_ref.shape[1]), F32)
        state_ref[...] = jnp.zeros(state_ref.shape, F32)

    xpad_ref[8:8 + cs, :] = xbc_ref[...]
    conv = cb_ref[...]
    for k in range(SSD_CONV):
        lo = 8 - (SSD_CONV - 1) + k
        conv = conv + cw_ref[k:k + 1, :] * xpad_ref[lo:lo + cs, :]
    xpad_ref[0:8, :] = xpad_ref[cs:cs + 8, :]
    xc = _silu(conv)
    xs = xc[:, :inner]
    bm = xc[:, inner:inner + SSD_GROUPS * SSD_STATE]
    cm = xc[:, inner + SSD_GROUPS * SSD_STATE:]

    dt = _softplus(dt_ref[...] + dtb_ref[...])
    da = dt * -jnp.exp(alog_ref[...])
    a_cum = _dot_select(tril_ref[...], da)
    a_cum_t = a_cum.T
    expand = expand_ref[...]
    xdt = xs * _select_dot(dt, expand)
    a_x = _select_dot(a_cum, expand)
    a_last = a_x[cs - 1:cs, :]
    decay_in = jnp.exp(a_x)
    xd = (xdt * jnp.exp(a_last - a_x)).astype(BF16)
    xdt = xdt.astype(BF16)

    state = state_ref[...]
    y = _dot(cm.astype(BF16), state.astype(BF16)) * decay_in
    new_state = state * jnp.exp(a_last) + _dot(bm.T.astype(BF16), xd)
    state_ref[...] = new_state * bdmask_ref[...]

    row = lax.broadcasted_iota(jnp.int32, (cs, cs), 0)
    col = lax.broadcasted_iota(jnp.int32, (cs, cs), 1)
    causal = row >= col
    bm16 = bm.astype(BF16)
    ys = []
    for g in range(SSD_GROUPS):
        cb = _dot_nt((cm * gmask_ref[g:g + 1, :]).astype(BF16), bm16)
        xg = xdt[:, g * gw:(g + 1) * gw]
        for j in range(hpg):
            h = g * hpg + j
            diff = a_cum[:, h:h + 1] - a_cum_t[h:h + 1, :]
            seg = jnp.exp(jnp.where(causal, diff, -jnp.inf))
            lhs_ref[:, j * cs:(j + 1) * cs] = (cb * seg).astype(BF16)
            rhs_ref[j * cs:(j + 1) * cs, :] = xg * hmask_ref[j:j + 1, :]
        ys.append(_dot(lhs_ref[...], rhs_ref[...]))
    y = y + jnp.concatenate(ys, axis=1) + dsk_ref[...] * xs
    y = y * _silu(z_ref[...])
    o_ref[...] = _rms(y, ng_ref[...]).astype(o_ref.dtype)


def ssd_mixer(z, xbc, dtp, conv_w, conv_b, dt_bias, a_log, d_skip, norm_g, batch, seqlen):
    cs = SSD_TILE
    inner = SSD_HEADS * SSD_HEADDIM
    gw = inner // SSD_GROUPS
    gs = SSD_GROUPS * SSD_STATE
    hpg = SSD_HEADS // SSD_GROUPS
    nt = seqlen // cs
    assert seqlen % cs == 0 and z.shape == (batch * seqlen, inner)

    pad = lambda v: jnp.pad(v.astype(F32), (0, LANES - SSD_HEADS)).reshape(1, LANES)
    lane_head = jnp.arange(inner) // SSD_HEADDIM
    expand = (jnp.arange(LANES)[:, None] == lane_head[None, :]).astype(BF16)
    tril = (jnp.arange(cs)[:, None] >= jnp.arange(cs)[None, :]).astype(BF16)
    hmask = (jnp.arange(hpg)[:, None] == (jnp.arange(gw) // SSD_HEADDIM)[None, :]).astype(BF16)
    gmask = (jnp.arange(SSD_GROUPS)[:, None] == (jnp.arange(gs) // SSD_STATE)[None, :]).astype(F32)
    bdmask = ((jnp.arange(gs) // SSD_STATE)[:, None] == (jnp.arange(inner) // gw)[None, :]).astype(F32)

    tok = lambda n: pl.BlockSpec((cs, n), lambda b, c: (b * nt + c, 0))
    consts = [conv_w.astype(F32), conv_b.astype(F32).reshape(1, -1), pad(dt_bias), pad(a_log),
              jnp.repeat(d_skip.astype(F32), SSD_HEADDIM).reshape(1, inner), norm_g.astype(F32).reshape(1, inner),
              expand, tril, hmask, gmask, bdmask]
    return pl.pallas_call(
        _ssd_kernel, grid=(batch, nt),
        in_specs=[tok(inner), tok(xbc.shape[1]), tok(LANES)] + [_resident(c.shape) for c in consts],
        out_specs=tok(inner),
        out_shape=jax.ShapeDtypeStruct((batch * seqlen, inner), BF16),
        scratch_shapes=[pltpu.VMEM((cs + 8, xbc.shape[1]), F32), pltpu.VMEM((gs, inner), F32),
                        pltpu.VMEM((cs, hpg * cs), BF16), pltpu.VMEM((hpg * cs, gw), BF16)],
        compiler_params=_params("parallel", "arbitrary"), name="ssd_mixer",
    )(z, xbc, dtp, *consts)


S5_TILE = 64
S5_GROUP = 16
S5_STATE = 64
S5_GROUPS_PER_BLOCK = 16
SUBLANES = 8


def _s5_kernel(u_ref, wb_ref, a_ref, wc_ref, d_ref, gw_ref, gb_ref, o_ref, bu_ref, x_ref):
    bsz, tl, width = u_ref.shape
    nblk = wb_ref.shape[0]
    kw = wb_ref.shape[1]
    sw = a_ref.shape[2]
    rows = tl * bsz

    @pl.when(pl.program_id(0) == 0)
    def _():
        x_ref[...] = jnp.zeros(x_ref.shape, F32)

    u = jnp.swapaxes(u_ref[...], 0, 1).reshape(rows, width)
    u16 = u.astype(BF16)
    for kb in range(nblk):
        bu_ref[:, kb * 2 * sw:(kb + 1) * 2 * sw] = _dot(u16[:, kb * kw:(kb + 1) * kw], wb_ref[kb])

    for kb in range(nblk):
        re = slice(kb * 2 * sw, kb * 2 * sw + sw)
        im = slice(kb * 2 * sw + sw, (kb + 1) * 2 * sw)
        a_re = jnp.broadcast_to(a_ref[kb, 0:1, :], (bsz, sw))
        a_im = jnp.broadcast_to(a_ref[kb, 1:2, :], (bsz, sw))

        def step(t, carry, re=re, im=im, a_re=a_re, a_im=a_im):
            x_re, x_im = carry
            r = pl.ds(pl.multiple_of(t * bsz, bsz), bsz)
            n_re = a_re * x_re - a_im * x_im + bu_ref[r, re]
            n_im = a_re * x_im + a_im * x_re + bu_ref[r, im]
            bu_ref[r, re] = n_re
            bu_ref[r, im] = n_im
            return n_re, n_im

        x_re, x_im = lax.fori_loop(0, tl, step, (x_ref[:, re], x_ref[:, im]), unroll=4)
        x_ref[:, re] = x_re
        x_ref[:, im] = x_im

    ys = []
    for kb in range(nblk):
        xs16 = bu_ref[:, kb * 2 * sw:(kb + 1) * 2 * sw].astype(BF16)
        ys.append(_dot(xs16, wc_ref[kb]))
    y = jnp.concatenate(ys, axis=1) + d_ref[...] * u
    z = jax.nn.gelu(y)
    out = z * jax.nn.sigmoid(_dot(z.astype(BF16), gw_ref[...]) + gb_ref[...])
    o_ref[...] = jnp.swapaxes(out.reshape(tl, bsz, width), 0, 1).astype(o_ref.dtype)


def _s5_weights(lam_re, lam_im, log_step, b_re, b_im, c_re, c_im):
    lr, li = lam_re.astype(F32), lam_im.astype(F32)
    step = jnp.exp(log_step.astype(F32))[:, None]
    mag = jnp.exp(lr * step)
    a_re, a_im = mag * jnp.cos(li * step), mag * jnp.sin(li * step)
    den = lr * lr + li * li
    k_re = ((a_re - 1.0) * lr + a_im * li) / den
    k_im = (a_im * lr - (a_re - 1.0) * li) / den
    br, bi = b_re.astype(F32), b_im.astype(F32)
    bb_re = k_re[..., None] * br - k_im[..., None] * bi
    bb_im = k_re[..., None] * bi + k_im[..., None] * br
    groups, nstate, nch = br.shape
    gpb = S5_GROUPS_PER_BLOCK
    nblk = groups // gpb
    eye = jnp.eye(gpb, dtype=F32)

    def in_block(bb):
        t = bb.reshape(nblk, gpb, nstate, nch).transpose(0, 1, 3, 2)
        t = t[:, :, :, None, :] * eye[None, :, None, :, None]
        return t.reshape(nblk, gpb * nch, gpb * nstate)

    def out_block(c):
        t = c.astype(F32).reshape(nblk, gpb, nch, nstate).transpose(0, 1, 3, 2)
        t = t[:, :, :, None, :] * eye[None, :, None, :, None]
        return t.reshape(nblk, gpb * nstate, gpb * nch)

    wb = jnp.concatenate([in_block(bb_re), in_block(bb_im)], axis=2).astype(BF16)
    wc = jnp.concatenate([out_block(c_re), -out_block(c_im)], axis=1).astype(BF16)
    a = jnp.stack([a_re.reshape(nblk, gpb * nstate), a_im.reshape(nblk, gpb * nstate)], axis=1)
    return wb, a, wc


def s5_mixer(u, lam_re, lam_im, log_step, b_re, b_im, c_re, c_im, d_skip, glu_w, glu_b):
    bsz, seqlen, width = u.shape
    tl = S5_TILE
    assert bsz == SUBLANES and seqlen % tl == 0
    wb, a, wc = _s5_weights(lam_re, lam_im, log_step, b_re, b_im, c_re, c_im)
    nstates = wb.shape[0] * wb.shape[2]
    consts = [wb, a, wc, d_skip.astype(F32).reshape(1, width), glu_w.astype(BF16),
              glu_b.astype(F32).reshape(1, width)]
    tok = pl.BlockSpec((bsz, tl, width), lambda i: (0, i, 0))
    return pl.pallas_call(
        _s5_kernel, grid=(seqlen // tl,),
        in_specs=[tok] + [_resident(c.shape) for c in consts],
        out_specs=tok,
        out_shape=jax.ShapeDtypeStruct((bsz, seqlen, width), BF16),
        scratch_shapes=[pltpu.VMEM((tl * bsz, nstates), F32), pltpu.VMEM((bsz, nstates), F32)],
        compiler_params=_params("arbitrary"), name="s5_mixer",
    )(u, *consts)


MIX_TILE = 256
MIX_HEADS = 4
MIX_QK = 128
MIX_V = 256
ROPE_BASE = 10000.0


def _dot_tn(a, b):
    return lax.dot_general(a, b, (((0,), (0,)), ((), ())), preferred_element_type=F32)


def _rotary(t, cos, sin_signed):
    return t * cos + pltpu.roll(t, MIX_QK // 2, axis=1) * sin_signed


def _ret_kernel(q_ref, k_ref, v_ref, g_ref, cos_ref, sin_ref, dmat_ref, qdec_ref, kdec_ref, cdec_ref,
                ng_ref, o_ref, state_ref):
    @pl.when(pl.program_id(1) == 0)
    def _():
        state_ref[...] = jnp.zeros(state_ref.shape, F32)

    cos, sin = cos_ref[...], sin_ref[...]
    scale = MIX_QK ** -0.5
    for h in range(MIX_HEADS):
        kl = slice(h * MIX_QK, (h + 1) * MIX_QK)
        vl = slice(h * MIX_V, (h + 1) * MIX_V)
        q = _rotary(q_ref[:, kl], cos, sin)
        k = _rotary(k_ref[:, kl], cos, sin) * scale
        v16 = v_ref[:, vl].astype(BF16)
        q16 = q.astype(BF16)
        inner = _dot_nt(q16, k.astype(BF16)) * dmat_ref[h]
        state = state_ref[h]
        out = _dot(inner.astype(BF16), v16) + _dot_nt((q * qdec_ref[h]).astype(BF16), state.astype(BF16))
        state_ref[h] = state * cdec_ref[h] + _dot_tn(v16, (k * kdec_ref[h]).astype(BF16))
        mu = jnp.mean(out, axis=-1, keepdims=True)
        cen = out - mu
        var = jnp.mean(cen * cen, axis=-1, keepdims=True)
        o = cen * lax.rsqrt(var + NORM_EPS) * ng_ref[:, vl]
        o_ref[:, vl] = (o * _silu(g_ref[:, vl])).astype(o_ref.dtype)


def _rope_tables(seqlen):
    half = MIX_QK // 2
    inv_freq = ROPE_BASE ** (-jnp.arange(half, dtype=F32) / half)
    ang = jnp.arange(seqlen, dtype=F32)[:, None] * inv_freq[None, :]
    cos, sin = jnp.cos(ang), jnp.sin(ang)
    return jnp.concatenate([cos, cos], axis=1), jnp.concatenate([-sin, sin], axis=1)


def retention_mixer(q, k, v, g, norm_g, batch, seqlen):
    c = MIX_TILE
    nt = seqlen // c
    kw, vw = MIX_HEADS * MIX_QK, MIX_HEADS * MIX_V
    assert seqlen % c == 0 and q.shape == (batch * seqlen, kw) and v.shape == (batch * seqlen, vw)
    cos, sin = _rope_tables(seqlen)
    log_gamma = jnp.log(1.0 - 2.0 ** (-5.0 - jnp.arange(MIX_HEADS, dtype=F32)))
    pos = jnp.arange(c, dtype=F32)
    diff = pos[:, None] - pos[None, :]
    dmat = jnp.where(diff >= 0, jnp.exp(log_gamma[:, None, None] * jnp.maximum(diff, 0.0)), 0.0)
    lanes = lambda col: jnp.broadcast_to(col[:, :, None], (MIX_HEADS, c, MIX_QK))
    qdec = lanes(jnp.exp(log_gamma[:, None] * (pos + 1.0)))
    kdec = lanes(jnp.exp(log_gamma[:, None] * (c - 1.0 - pos)))
    cdec = jnp.broadcast_to(jnp.exp(log_gamma * c)[:, None, None], (MIX_HEADS, 1, MIX_QK))
    consts = [dmat, qdec, kdec, cdec, norm_g.astype(F32).reshape(1, vw)]
    tok = lambda n: pl.BlockSpec((c, n), lambda b, i: (b * nt + i, 0))
    rope = pl.BlockSpec((c, MIX_QK), lambda b, i: (i, 0))
    return pl.pallas_call(
        _ret_kernel, grid=(batch, nt),
        in_specs=[tok(kw), tok(kw), tok(vw), tok(vw), rope, rope] + [_resident(x.shape) for x in consts],
        out_specs=tok(vw),
        out_shape=jax.ShapeDtypeStruct((batch * seqlen, vw), BF16),
        scratch_shapes=[pltpu.VMEM((MIX_HEADS, MIX_V, MIX_QK), F32)],
        compiler_params=_params("parallel", "arbitrary"), name="retention_mixer",
    )(q, k, v, g, cos, sin, *consts)


GLA_CHUNK = 64
GLA_TAU = 16.0


def _log_sigmoid(x):
    return jnp.minimum(x, 0.0) - jnp.log1p(jnp.exp(-jnp.abs(x)))


def _gla_kernel(q_ref, k_ref, v_ref, a_ref, g_ref, gw_ref, gb_ref, ng_ref, cumsel_ref, lastsel_ref,
                o_ref, state_ref):
    tl = q_ref.shape[0]
    nchunk = tl // GLA_CHUNK

    @pl.when(pl.program_id(1) == 0)
    def _():
        state_ref[...] = jnp.zeros(state_ref.shape, F32)

    log_a = _log_sigmoid(_dot(a_ref[...].astype(BF16), gw_ref[...]) + gb_ref[...]) / GLA_TAU
    bcum = _dot_select(cumsel_ref[...], log_a)
    blast = _dot_select(lastsel_ref[...], log_a)
    grow = jnp.exp(bcum)
    row = lax.broadcasted_iota(jnp.int32, (tl, tl), 0)
    col = lax.broadcasted_iota(jnp.int32, (tl, tl), 1)
    shift = GLA_CHUNK.bit_length() - 1
    keep = (row >= col) & (jnp.right_shift(row, shift) == jnp.right_shift(col, shift))
    col_chunk = jnp.right_shift(lax.broadcasted_iota(jnp.int32, (1, tl), 1), shift)
    scale = MIX_QK ** -0.5
    for h in range(MIX_HEADS):
        kl = slice(h * MIX_QK, (h + 1) * MIX_QK)
        vl = slice(h * MIX_V, (h + 1) * MIX_V)
        k = k_ref[:, kl]
        q_in = (q_ref[:, kl] * scale * grow[:, kl]).astype(BF16)
        k_in = (k * jnp.exp(-bcum[:, kl])).astype(BF16)
        k_out = (k * jnp.exp(blast[:, kl] - bcum[:, kl])).astype(BF16)
        v = v_ref[:, vl]
        v16 = v.astype(BF16)
        att = jnp.where(keep, _dot_nt(q_in, k_in), 0.0)
        out = _dot(att.astype(BF16), v16)
        v_t = v.T
        stacked = jnp.concatenate(
            [jnp.where(col_chunk == c, v_t, 0.0) for c in range(nchunk)], axis=0).astype(BF16)
        kv = _dot(stacked, k_out)
        state = state_ref[h]
        inter = []
        for c in range(nchunk):
            rows = slice(c * GLA_CHUNK, (c + 1) * GLA_CHUNK)
            inter.append(_dot_nt(q_in[rows, :], state.astype(BF16)))
            decay = jnp.exp(blast[c * GLA_CHUNK:c * GLA_CHUNK + 1, kl])
            state = state * decay + kv[c * MIX_V:(c + 1) * MIX_V, :]
        state_ref[h] = state
        out = out + jnp.concatenate(inter, axis=0)
        o = out * lax.rsqrt(jnp.mean(out * out, axis=-1, keepdims=True) + NORM_EPS) * ng_ref[:, vl]
        o_ref[:, vl] = (o * _silu(g_ref[:, vl])).astype(o_ref.dtype)


def gla_mixer(q, k, v, a_lr, g, gate_w, gate_b, norm_g, batch, seqlen):
    tl = MIX_TILE
    nt = seqlen // tl
    kw, vw = MIX_HEADS * MIX_QK, MIX_HEADS * MIX_V
    assert seqlen % tl == 0 and tl % GLA_CHUNK == 0 and a_lr.shape == (batch * seqlen, LANES)
    rank = gate_w.shape[0]
    gw = jnp.pad(gate_w.astype(F32), ((0, LANES - rank), (0, 0))).astype(BF16)
    t = jnp.arange(tl)
    same = (t[:, None] // GLA_CHUNK) == (t[None, :] // GLA_CHUNK)
    cumsel = (same & (t[:, None] >= t[None, :])).astype(BF16)
    lastsel = same.astype(BF16)
    consts = [gw, gate_b.astype(F32).reshape(1, kw), norm_g.astype(F32).reshape(1, vw), cumsel, lastsel]
    tok = lambda n: pl.BlockSpec((tl, n), lambda b, i: (b * nt + i, 0))
    return pl.pallas_call(
        _gla_kernel, grid=(batch, nt),
        in_specs=[tok(kw), tok(kw), tok(vw), tok(LANES), tok(vw)] + [_resident(x.shape) for x in consts],
        out_specs=tok(vw),
        out_shape=jax.ShapeDtypeStruct((batch * seqlen, vw), BF16),
        scratch_shapes=[pltpu.VMEM((MIX_HEADS, MIX_V, MIX_QK), F32)],
        compiler_params=_params("parallel", "arbitrary"), name="gla_mixer",
    )(q, k, v, a_lr, g, *consts)


def _split_weight(w, sizes, pad_to_lanes=()):
    slabs, start = [], 0
    for i, s in enumerate(sizes):
        slab = w[:, start:start + s]
        if i in pad_to_lanes:
            slab = jnp.pad(slab, ((0, 0), (0, LANES - s)))
        slabs.append(slab.astype(BF16))
        start += s
    assert start == w.shape[1]
    return slabs


def kernel(x, norm_mix_g, norm_ffn_g, final_norm_g, ev_in_w, s5_lam_re, s5_lam_im, s5_log_step, s5_b_re, s5_b_im, s5_c_re, s5_c_im, s5_d, s5_glu_w, s5_glu_b, ssd_conv_w, ssd_conv_b, ssd_dt_bias, ssd_a_log, ssd_d, ssd_norm_g, ev_out_w, od_in_w, ret_norm_g, gla_gate_w, gla_gate_b, gla_norm_g, od_out_w, ffn_gate_w, ffn_up_w, ffn_down_w):
    bsz, seqlen, d = x.shape
    rows = bsz * seqlen
    depth = norm_mix_g.shape[0]
    h = x.reshape(rows, d)
    for layer in range(depth):
        i = layer // 2
        if layer % 2 == 0:
            s5_w = s5_glu_w.shape[1]
            ssd_inner = ssd_norm_g.shape[1]
            conv_dim = ssd_conv_w.shape[2]
            heads = ssd_dt_bias.shape[1]
            slabs = _split_weight(ev_in_w[i], (s5_w, ssd_inner, conv_dim, heads), pad_to_lanes=(3,))
            u, z, xbc, dtp = norm_proj(h, norm_mix_g[layer], slabs, (F32,) * 4)
            y_a = s5_mixer(u.reshape(bsz, seqlen, s5_w), s5_lam_re[i], s5_lam_im[i], s5_log_step[i],
                           s5_b_re[i], s5_b_im[i], s5_c_re[i], s5_c_im[i], s5_d[i].reshape(-1),
                           s5_glu_w[i], s5_glu_b[i]).reshape(rows, s5_w)
            y_b = ssd_mixer(z, xbc, dtp, ssd_conv_w[i], ssd_conv_b[i], ssd_dt_bias[i], ssd_a_log[i],
                            ssd_d[i], ssd_norm_g[i], bsz, seqlen)
            w_out = ev_out_w[i].astype(BF16)
        else:
            kw = gla_gate_w.shape[2]
            vw = ret_norm_g.shape[1]
            rank = gla_gate_w.shape[1]
            slabs = _split_weight(od_in_w[i], (kw, kw, vw, vw, kw, kw, vw, rank, vw), pad_to_lanes=(7,))
            q_r, k_r, v_r, g_r, q_g, k_g, v_g, a_lr, g_g = norm_proj(h, norm_mix_g[layer], slabs, (F32,) * 9)
            y_a = retention_mixer(q_r, k_r, v_r, g_r, ret_norm_g[i], bsz, seqlen)
            y_b = gla_mixer(q_g, k_g, v_g, a_lr, g_g, gla_gate_w[i], gla_gate_b[i], gla_norm_g[i], bsz, seqlen)
            w_out = od_out_w[i].astype(BF16)
        na = y_a.shape[1]
        h = mix_ffn(h, y_a, y_b, w_out[:na], w_out[na:], norm_ffn_g[layer],
                    ffn_gate_w[layer].astype(BF16), ffn_up_w[layer].astype(BF16), ffn_down_w[layer].astype(BF16),
                    final_norm_g, final_norm=(layer == depth - 1))
    return h.reshape(bsz, seqlen, d)
```

```python
import functools
import math

import jax
import jax.numpy as jnp
from jax import lax
from jax.experimental import pallas as pl
from jax.experimental.pallas import tpu as pltpu

F32 = jnp.float32
BF16 = jnp.bfloat16

NORM_EPS = 1e-6
VMEM_LIMIT_BYTES = 56 * 1024 * 1024
LANES = 128
SUBLANES = 8

ROW_TILE = 512
FFN_CHUNK = 256


def _rms(x, g):
    return x * lax.rsqrt(jnp.mean(x * x, axis=-1, keepdims=True) + NORM_EPS) * g


def _dot(a, b):
    return jnp.dot(a, b, preferred_element_type=F32)


def _resident(shape):
    zeros = (0,) * len(shape)
    return pl.BlockSpec(shape, lambda *_: zeros, pipeline_mode=pl.Buffered(1))


def _params(*semantics):
    return pltpu.CompilerParams(dimension_semantics=semantics,
                                vmem_limit_bytes=VMEM_LIMIT_BYTES)


def _norm_proj_kernel(x_ref, g_ref, w_ref, *o_refs):
    xn = _rms(x_ref[...], g_ref[...]).astype(BF16)
    start = 0
    for o_ref in o_refs:
        n = o_ref.shape[1]
        o_ref[...] = _dot(xn, w_ref[:, start:start + n]).astype(o_ref.dtype)
        start += n


def norm_proj(x, g, w, widths, out_dtypes, row_tile=ROW_TILE):
    rows, d = x.shape
    assert rows % row_tile == 0 and sum(widths) == w.shape[1] and all(n % LANES == 0 for n in widths)
    return pl.pallas_call(
        _norm_proj_kernel, grid=(rows // row_tile,),
        in_specs=[pl.BlockSpec((row_tile, d), lambda i: (i, 0)), _resident((1, d)), _resident(w.shape)],
        out_specs=[pl.BlockSpec((row_tile, n), lambda i: (i, 0)) for n in widths],
        out_shape=[jax.ShapeDtypeStruct((rows, n), dt) for n, dt in zip(widths, out_dtypes)],
        compiler_params=_params("parallel"), name="norm_proj",
    )(x, g.reshape(1, d), w)


def _mix_ffn_kernel(h_ref, ya_ref, yb_ref, wo_ref, g_ref, wg_ref, wu_ref, wd_ref,
                    fg_ref, o_ref, act_ref, *, final_norm):
    na = ya_ref.shape[1]
    h1 = h_ref[...] + _dot(ya_ref[...], wo_ref[:na, :]) + _dot(yb_ref[...], wo_ref[na:, :])
    xn = _rms(h1, g_ref[...]).astype(BF16)
    hidden = wg_ref.shape[1]
    for j in range(hidden // FFN_CHUNK):
        cols = slice(j * FFN_CHUNK, (j + 1) * FFN_CHUNK)
        gate = _dot(xn, wg_ref[:, cols])
        up = _dot(xn, wu_ref[:, cols])
        act_ref[:, cols] = (gate * jax.nn.sigmoid(gate) * up).astype(BF16)
    out = h1 + _dot(act_ref[...], wd_ref[...])
    if final_norm:
        out = _rms(out, fg_ref[...])
    o_ref[...] = out


def _resident_layer(shape, layer):
    tail = (0,) * (len(shape) - 1)
    return pl.BlockSpec((None,) + tuple(shape[1:]), lambda *_: (layer,) + tail, pipeline_mode=pl.Buffered(1))


def mix_ffn(h, ya, yb, wo, g, wg, wu, wd, layer, fg, final_norm, row_tile=ROW_TILE):
    rows, d = h.shape
    hidden = wg.shape[2]
    assert rows % row_tile == 0 and hidden % FFN_CHUNK == 0
    assert wo.shape == (ya.shape[1] + yb.shape[1], d)
    row = lambda n: pl.BlockSpec((row_tile, n), lambda i: (i, 0))
    return pl.pallas_call(
        functools.partial(_mix_ffn_kernel, final_norm=final_norm),
        grid=(rows // row_tile,),
        in_specs=[row(d), row(ya.shape[1]), row(yb.shape[1]), _resident(wo.shape), _resident((1, d)),
                  _resident_layer(wg.shape, layer), _resident_layer(wu.shape, layer),
                  _resident_layer(wd.shape, layer), _resident((1, d))],
        out_specs=row(d),
        out_shape=jax.ShapeDtypeStruct((rows, d), F32),
        scratch_shapes=[pltpu.VMEM((row_tile, hidden), BF16)],
        compiler_params=_params("parallel"), name="mix_ffn",
    )(h, ya, yb, wo, g.reshape(1, d), wg, wu, wd, fg.reshape(1, d))


def _split3(a):
    hi = a.astype(BF16)
    r1 = a - hi.astype(F32)
    mid = r1.astype(BF16)
    lo = (r1 - mid.astype(F32)).astype(BF16)
    return hi, mid, lo


def _select_dot(a, sel):
    hi, mid, lo = _split3(a)
    return _dot(hi, sel) + _dot(mid, sel) + _dot(lo, sel)


def _dot_select(sel, a):
    hi, mid, lo = _split3(a)
    return _dot(sel, hi) + _dot(sel, mid) + _dot(sel, lo)


def _dot_nt(a, b):
    return lax.dot_general(a, b, (((1,), (1,)), ((), ())), preferred_element_type=F32)


def _silu(x):
    return x * jax.nn.sigmoid(x)


def _softplus(x):
    return jnp.maximum(x, 0.0) + jnp.log1p(jnp.exp(-jnp.abs(x)))


SSD_TILE = 256
SSD_HEADS = 16
SSD_HEADDIM = 64
SSD_GROUPS = 4
SSD_STATE = 64
SSD_CONV = 4


def _ssd_kernel(z_ref, xbc_ref, dt_ref, cw_ref, cb_ref, dtb_ref, alog_ref, dsk_ref, ng_ref,
                expand_ref, tril_ref, hmask_ref, gmask_ref, bdmask_ref,
                o_ref, xpad_ref, state_ref, lhs_ref, rhs_ref):
    cs = z_ref.shape[0]
    inner = z_ref.shape[1]
    gw = inner // SSD_GROUPS
    hpg = SSD_HEADS // SSD_GROUPS

    nslab = xpad_ref.shape[0]

    @pl.when(pl.program_id(1) == 0)
    def _():
        xpad_ref[:, 0:8, :] = jnp.zeros((nslab, 8, LANES), F32)
        state_ref[...] = jnp.zeros(state_ref.shape, F32)

    convs = []
    for j in range(nslab):
        lanes = slice(j * LANES, (j + 1) * LANES)
        xpad_ref[j, 8:8 + cs, :] = xbc_ref[:, lanes]
        conv = cb_ref[:, lanes]
        for k in range(SSD_CONV):
            lo = 8 - (SSD_CONV - 1) + k
            conv = conv + cw_ref[k:k + 1, lanes] * xpad_ref[j, lo:lo + cs, :]
        xpad_ref[j, 0:8, :] = xpad_ref[j, cs:cs + 8, :]
        convs.append(conv)
    xc = _silu(jnp.concatenate(convs, axis=1))
    xs = xc[:, :inner]
    bm = xc[:, inner:inner + SSD_GROUPS * SSD_STATE]
    cm = xc[:, inner + SSD_GROUPS * SSD_STATE:]

    dt = _softplus(dt_ref[...] + dtb_ref[...])
    da = dt * -jnp.exp(alog_ref[...])
    a_cum = _dot_select(tril_ref[...], da)
    a_cum_t = a_cum.T
    expand = expand_ref[...]
    xdt = xs * _select_dot(dt, expand)
    a_x = _select_dot(a_cum, expand)
    a_last = a_x[cs - 1:cs, :]
    decay_in = jnp.exp(a_x)
    xd = (xdt * jnp.exp(a_last - a_x)).astype(BF16)
    xdt = xdt.astype(BF16)

    state = state_ref[...]
    y = _dot(cm.astype(BF16), state.astype(BF16)) * decay_in
    new_state = state * jnp.exp(a_last) + _dot(bm.T.astype(BF16), xd)
    state_ref[...] = new_state * bdmask_ref[...]

    row = lax.broadcasted_iota(jnp.int32, (cs, cs), 0)
    col = lax.broadcasted_iota(jnp.int32, (cs, cs), 1)
    causal = row >= col
    bm16 = bm.astype(BF16)
    ys = []
    for g in range(SSD_GROUPS):
        cb = _dot_nt((cm * gmask_ref[g:g + 1, :]).astype(BF16), bm16)
        xg = xdt[:, g * gw:(g + 1) * gw]
        for j in range(hpg):
            h = g * hpg + j
            diff = a_cum[:, h:h + 1] - a_cum_t[h:h + 1, :]
            seg = jnp.exp(jnp.where(causal, diff, -jnp.inf))
            lhs_ref[:, j * cs:(j + 1) * cs] = (cb * seg).astype(BF16)
            rhs_ref[j * cs:(j + 1) * cs, :] = xg * hmask_ref[j:j + 1, :]
        ys.append(_dot(lhs_ref[...], rhs_ref[...]))
    y = y + jnp.concatenate(ys, axis=1) + dsk_ref[...] * xs
    y = y * _silu(z_ref[...])
    o_ref[...] = _rms(y, ng_ref[...]).astype(o_ref.dtype)


def ssd_mixer(z, xbc, dtp, conv_w, conv_b, dt_bias, a_log, d_skip, norm_g, batch, seqlen):
    cs = SSD_TILE
    inner = SSD_HEADS * SSD_HEADDIM
    gw = inner // SSD_GROUPS
    gs = SSD_GROUPS * SSD_STATE
    hpg = SSD_HEADS // SSD_GROUPS
    nt = seqlen // cs
    assert seqlen % cs == 0 and z.shape == (batch * seqlen, inner)

    pad = lambda v: jnp.pad(v.astype(F32), (0, LANES - SSD_HEADS)).reshape(1, LANES)
    lane_head = jnp.arange(inner) // SSD_HEADDIM
    expand = (jnp.arange(LANES)[:, None] == lane_head[None, :]).astype(BF16)
    tril = (jnp.arange(cs)[:, None] >= jnp.arange(cs)[None, :]).astype(BF16)
    hmask = (jnp.arange(hpg)[:, None] == (jnp.arange(gw) // SSD_HEADDIM)[None, :]).astype(BF16)
    gmask = (jnp.arange(SSD_GROUPS)[:, None] == (jnp.arange(gs) // SSD_STATE)[None, :]).astype(F32)
    bdmask = ((jnp.arange(gs) // SSD_STATE)[:, None] == (jnp.arange(inner) // gw)[None, :]).astype(F32)

    tok = lambda n: pl.BlockSpec((cs, n), lambda b, c: (b * nt + c, 0))
    consts = [conv_w.astype(F32), conv_b.astype(F32).reshape(1, -1), pad(dt_bias), pad(a_log),
              jnp.repeat(d_skip.astype(F32), SSD_HEADDIM).reshape(1, inner), norm_g.astype(F32).reshape(1, inner),
              expand, tril, hmask, gmask, bdmask]
    return pl.pallas_call(
        _ssd_kernel, grid=(batch, nt),
        in_specs=[tok(inner), tok(xbc.shape[1]), tok(LANES)] + [_resident(c.shape) for c in consts],
        out_specs=tok(inner),
        out_shape=jax.ShapeDtypeStruct((batch * seqlen, inner), BF16),
        scratch_shapes=[pltpu.VMEM((xbc.shape[1] // LANES, cs + 8, LANES), F32), pltpu.VMEM((gs, inner), F32),
                        pltpu.VMEM((cs, hpg * cs), BF16), pltpu.VMEM((hpg * cs, gw), BF16)],
        compiler_params=_params("parallel", "arbitrary"), name="ssd_mixer",
    )(z, xbc, dtp, *consts)


S5_TILE = 64
S5_GROUP = 16
S5_STATE = 64
S5_GROUPS_PER_BLOCK = 16


def _s5_kernel(u_ref, wb_ref, a_ref, wc_ref, d_ref, gw_ref, gb_ref, o_ref, bu_ref, x_ref):
    bsz, tl, width = u_ref.shape
    nblk = wb_ref.shape[0]
    kw = wb_ref.shape[1]
    sw = a_ref.shape[2]
    rows = tl * bsz

    @pl.when(pl.program_id(0) == 0)
    def _():
        x_ref[...] = jnp.zeros(x_ref.shape, F32)

    u = jnp.swapaxes(u_ref[...], 0, 1).reshape(rows, width)
    u16 = u.astype(BF16)
    for kb in range(nblk):
        bu_ref[:, kb * 2 * sw:(kb + 1) * 2 * sw] = _dot(u16[:, kb * kw:(kb + 1) * kw], wb_ref[kb])

    for kb in range(nblk):
        re = slice(kb * 2 * sw, kb * 2 * sw + sw)
        im = slice(kb * 2 * sw + sw, (kb + 1) * 2 * sw)
        a_re = jnp.broadcast_to(a_ref[kb, 0:1, :], (bsz, sw))
        a_im = jnp.broadcast_to(a_ref[kb, 1:2, :], (bsz, sw))

        def step(t, carry, re=re, im=im, a_re=a_re, a_im=a_im):
            x_re, x_im = carry
            r = pl.ds(pl.multiple_of(t * bsz, bsz), bsz)
            n_re = a_re * x_re - a_im * x_im + bu_ref[r, re]
            n_im = a_re * x_im + a_im * x_re + bu_ref[r, im]
            bu_ref[r, re] = n_re
            bu_ref[r, im] = n_im
            return n_re, n_im

        x_re, x_im = lax.fori_loop(0, tl, step, (x_ref[:, re], x_ref[:, im]), unroll=4)
        x_ref[:, re] = x_re
        x_ref[:, im] = x_im

    ys = []
    for kb in range(nblk):
        xs16 = bu_ref[:, kb * 2 * sw:(kb + 1) * 2 * sw].astype(BF16)
        ys.append(_dot(xs16, wc_ref[kb]))
    y = jnp.concatenate(ys, axis=1) + d_ref[...] * u
    z = jax.nn.gelu(y)
    out = z * jax.nn.sigmoid(_dot(z.astype(BF16), gw_ref[...]) + gb_ref[...])
    o_ref[...] = jnp.swapaxes(out.reshape(tl, bsz, width), 0, 1).astype(o_ref.dtype)


def _s5_weights(lam_re, lam_im, log_step, b_re, b_im, c_re, c_im):
    lr, li = lam_re.astype(F32), lam_im.astype(F32)
    step = jnp.exp(log_step.astype(F32))[:, None]
    mag = jnp.exp(lr * step)
    a_re, a_im = mag * jnp.cos(li * step), mag * jnp.sin(li * step)
    den = lr * lr + li * li
    k_re = ((a_re - 1.0) * lr + a_im * li) / den
    k_im = (a_im * lr - (a_re - 1.0) * li) / den
    br, bi = b_re.astype(F32), b_im.astype(F32)
    bb_re = k_re[..., None] * br - k_im[..., None] * bi
    bb_im = k_re[..., None] * bi + k_im[..., None] * br
    groups, nstate, nch = br.shape
    gpb = S5_GROUPS_PER_BLOCK
    nblk = groups // gpb
    eye = jnp.eye(gpb, dtype=F32)

    def in_block(bb):
        t = bb.reshape(nblk, gpb, nstate, nch).transpose(0, 1, 3, 2)
        t = t[:, :, :, None, :] * eye[None, :, None, :, None]
        return t.reshape(nblk, gpb * nch, gpb * nstate)

    def out_block(c):
        t = c.astype(F32).reshape(nblk, gpb, nch, nstate).transpose(0, 1, 3, 2)
        t = t[:, :, :, None, :] * eye[None, :, None, :, None]
        return t.reshape(nblk, gpb * nstate, gpb * nch)

    wb = jnp.concatenate([in_block(bb_re), in_block(bb_im)], axis=2).astype(BF16)
    wc = jnp.concatenate([out_block(c_re), -out_block(c_im)], axis=1).astype(BF16)
    a = jnp.stack([a_re.reshape(nblk, gpb * nstate), a_im.reshape(nblk, gpb * nstate)], axis=1)
    return wb, a, wc


def s5_mixer(u, lam_re, lam_im, log_step, b_re, b_im, c_re, c_im, d_skip, glu_w, glu_b):
    bsz, seqlen, width = u.shape
    tl = S5_TILE
    assert bsz == SUBLANES and seqlen % tl == 0
    wb, a, wc = _s5_weights(lam_re, lam_im, log_step, b_re, b_im, c_re, c_im)
    nstates = wb.shape[0] * wb.shape[2]
    consts = [wb, a, wc, d_skip.astype(F32).reshape(1, width), glu_w.astype(BF16),
              glu_b.astype(F32).reshape(1, width)]
    tok = pl.BlockSpec((bsz, tl, width), lambda i: (0, i, 0))
    return pl.pallas_call(
        _s5_kernel, grid=(seqlen // tl,),
        in_specs=[tok] + [_resident(c.shape) for c in consts],
        out_specs=tok,
        out_shape=jax.ShapeDtypeStruct((bsz, seqlen, width), BF16),
        scratch_shapes=[pltpu.VMEM((tl * bsz, nstates), F32), pltpu.VMEM((bsz, nstates), F32)],
        compiler_params=_params("arbitrary"), name="s5_mixer",
    )(u, *consts)


MIX_TILE = 256
MIX_HEADS = 4
MIX_QK = 128
MIX_V = 256
ROPE_BASE = 10000.0


def _dot_tn(a, b):
    return lax.dot_general(a, b, (((0,), (0,)), ((), ())), preferred_element_type=F32)


def _rotary(t, cos, sin_signed):
    return t * cos + pltpu.roll(t, MIX_QK // 2, axis=1) * sin_signed


def _ret_kernel(q_ref, k_ref, v_ref, g_ref, cos_ref, sin_ref, dmat_ref, qdec_ref, kdec_ref, cdec_ref,
                ng_ref, o_ref, state_ref):
    @pl.when(pl.program_id(1) == 0)
    def _():
        state_ref[...] = jnp.zeros(state_ref.shape, F32)

    cos, sin = cos_ref[...], sin_ref[...]
    scale = MIX_QK ** -0.5
    for h in range(MIX_HEADS):
        kl = slice(h * MIX_QK, (h + 1) * MIX_QK)
        vl = slice(h * MIX_V, (h + 1) * MIX_V)
        q = _rotary(q_ref[:, kl], cos, sin)
        k = _rotary(k_ref[:, kl], cos, sin) * scale
        v16 = v_ref[:, vl].astype(BF16)
        q16 = q.astype(BF16)
        inner = _dot_nt(q16, k.astype(BF16)) * dmat_ref[h]
        state = state_ref[h]
        out = _dot(inner.astype(BF16), v16) + _dot_nt((q * qdec_ref[h]).astype(BF16), state.astype(BF16))
        state_ref[h] = state * cdec_ref[h] + _dot_tn(v16, (k * kdec_ref[h]).astype(BF16))
        mu = jnp.mean(out, axis=-1, keepdims=True)
        cen = out - mu
        var = jnp.mean(cen * cen, axis=-1, keepdims=True)
        o = cen * lax.rsqrt(var + NORM_EPS) * ng_ref[:, vl]
        o_ref[:, vl] = (o * _silu(g_ref[:, vl])).astype(o_ref.dtype)


def _rope_tables(seqlen):
    half = MIX_QK // 2
    inv_freq = ROPE_BASE ** (-jnp.arange(half, dtype=F32) / half)
    ang = jnp.arange(seqlen, dtype=F32)[:, None] * inv_freq[None, :]
    cos, sin = jnp.cos(ang), jnp.sin(ang)
    return jnp.concatenate([cos, cos], axis=1), jnp.concatenate([-sin, sin], axis=1)


def retention_mixer(q, k, v, g, norm_g, batch, seqlen):
    c = MIX_TILE
    nt = seqlen // c
    kw, vw = MIX_HEADS * MIX_QK, MIX_HEADS * MIX_V
    assert seqlen % c == 0 and q.shape == (batch * seqlen, kw) and v.shape == (batch * seqlen, vw)
    cos, sin = _rope_tables(seqlen)
    log_gamma = jnp.log(1.0 - 2.0 ** (-5.0 - jnp.arange(MIX_HEADS, dtype=F32)))
    pos = jnp.arange(c, dtype=F32)
    diff = pos[:, None] - pos[None, :]
    dmat = jnp.where(diff >= 0, jnp.exp(log_gamma[:, None, None] * jnp.maximum(diff, 0.0)), 0.0)
    lanes = lambda col: jnp.broadcast_to(col[:, :, None], (MIX_HEADS, c, MIX_QK))
    qdec = lanes(jnp.exp(log_gamma[:, None] * (pos + 1.0)))
    kdec = lanes(jnp.exp(log_gamma[:, None] * (c - 1.0 - pos)))
    cdec = jnp.broadcast_to(jnp.exp(log_gamma * c)[:, None, None], (MIX_HEADS, 1, MIX_QK))
    consts = [dmat, qdec, kdec, cdec, norm_g.astype(F32).reshape(1, vw)]
    tok = lambda n: pl.BlockSpec((c, n), lambda b, i: (b * nt + i, 0))
    rope = pl.BlockSpec((c, MIX_QK), lambda b, i: (i, 0))
    return pl.pallas_call(
        _ret_kernel, grid=(batch, nt),
        in_specs=[tok(kw), tok(kw), tok(vw), tok(vw), rope, rope] + [_resident(x.shape) for x in consts],
        out_specs=tok(vw),
        out_shape=jax.ShapeDtypeStruct((batch * seqlen, vw), BF16),
        scratch_shapes=[pltpu.VMEM((MIX_HEADS, MIX_V, MIX_QK), F32)],
        compiler_params=_params("parallel", "arbitrary"), name="retention_mixer",
    )(q, k, v, g, cos, sin, *consts)


GLA_CHUNK = 64
GLA_TAU = 16.0


def _log_sigmoid(x):
    return jnp.minimum(x, 0.0) - jnp.log1p(jnp.exp(-jnp.abs(x)))


def _gla_kernel(q_ref, k_ref, v_ref, a_ref, g_ref, gw_ref, gb_ref, ng_ref, cumsel_ref, lastsel_ref,
                o_ref, state_ref):
    tl = q_ref.shape[0]
    nchunk = tl // GLA_CHUNK

    @pl.when(pl.program_id(1) == 0)
    def _():
        state_ref[...] = jnp.zeros(state_ref.shape, F32)

    log_a = _log_sigmoid(_dot(a_ref[...].astype(BF16), gw_ref[...]) + gb_ref[...]) / GLA_TAU
    bcum = _dot_select(cumsel_ref[...], log_a)
    blast = _dot_select(lastsel_ref[...], log_a)
    grow = jnp.exp(bcum)
    row = lax.broadcasted_iota(jnp.int32, (tl, tl), 0)
    col = lax.broadcasted_iota(jnp.int32, (tl, tl), 1)
    shift = GLA_CHUNK.bit_length() - 1
    keep = (row >= col) & (jnp.right_shift(row, shift) == jnp.right_shift(col, shift))
    col_chunk = jnp.right_shift(lax.broadcasted_iota(jnp.int32, (1, tl), 1), shift)
    scale = MIX_QK ** -0.5
    for h in range(MIX_HEADS):
        kl = slice(h * MIX_QK, (h + 1) * MIX_QK)
        vl = slice(h * MIX_V, (h + 1) * MIX_V)
        k = k_ref[:, kl]
        q_in = (q_ref[:, kl] * scale * grow[:, kl]).astype(BF16)
        k_in = (k * jnp.exp(-bcum[:, kl])).astype(BF16)
        k_out = (k * jnp.exp(blast[:, kl] - bcum[:, kl])).astype(BF16)
        v = v_ref[:, vl]
        v16 = v.astype(BF16)
        att = jnp.where(keep, _dot_nt(q_in, k_in), 0.0)
        out = _dot(att.astype(BF16), v16)
        v_t = v.T
        stacked = jnp.concatenate(
            [jnp.where(col_chunk == c, v_t, 0.0) for c in range(nchunk)], axis=0).astype(BF16)
        kv = _dot(stacked, k_out)
        state = state_ref[h]
        inter = []
        for c in range(nchunk):
            rows = slice(c * GLA_CHUNK, (c + 1) * GLA_CHUNK)
            inter.append(_dot_nt(q_in[rows, :], state.astype(BF16)))
            decay = jnp.exp(blast[c * GLA_CHUNK:c * GLA_CHUNK + 1, kl])
            state = state * decay + kv[c * MIX_V:(c + 1) * MIX_V, :]
        state_ref[h] = state
        out = out + jnp.concatenate(inter, axis=0)
        o = out * lax.rsqrt(jnp.mean(out * out, axis=-1, keepdims=True) + NORM_EPS) * ng_ref[:, vl]
        o_ref[:, vl] = (o * _silu(g_ref[:, vl])).astype(o_ref.dtype)


def gla_mixer(q, k, v, a_lr, g, gate_w, gate_b, norm_g, batch, seqlen):
    tl = MIX_TILE
    nt = seqlen // tl
    kw, vw = MIX_HEADS * MIX_QK, MIX_HEADS * MIX_V
    assert seqlen % tl == 0 and tl % GLA_CHUNK == 0 and a_lr.shape == (batch * seqlen, LANES)
    rank = gate_w.shape[0]
    gw = jnp.pad(gate_w.astype(F32), ((0, LANES - rank), (0, 0))).astype(BF16)
    t = jnp.arange(tl)
    same = (t[:, None] // GLA_CHUNK) == (t[None, :] // GLA_CHUNK)
    cumsel = (same & (t[:, None] >= t[None, :])).astype(BF16)
    lastsel = same.astype(BF16)
    consts = [gw, gate_b.astype(F32).reshape(1, kw), norm_g.astype(F32).reshape(1, vw), cumsel, lastsel]
    tok = lambda n: pl.BlockSpec((tl, n), lambda b, i: (b * nt + i, 0))
    return pl.pallas_call(
        _gla_kernel, grid=(batch, nt),
        in_specs=[tok(kw), tok(kw), tok(vw), tok(LANES), tok(vw)] + [_resident(x.shape) for x in consts],
        out_specs=tok(vw),
        out_shape=jax.ShapeDtypeStruct((batch * seqlen, vw), BF16),
        scratch_shapes=[pltpu.VMEM((MIX_HEADS, MIX_V, MIX_QK), F32)],
        compiler_params=_params("parallel", "arbitrary"), name="gla_mixer",
    )(q, k, v, a_lr, g, *consts)


def _narrow_last(w, sizes, narrow):
    starts = [sum(sizes[:i]) for i in range(len(sizes))]
    a, n = starts[narrow], sizes[narrow]
    assert sum(sizes) == w.shape[1] and n < LANES and all(s % LANES == 0 for i, s in enumerate(sizes) if i != narrow)
    parts = [w[:, :a], w[:, a + n:], w[:, a:a + n], jnp.zeros((w.shape[0], LANES - n), w.dtype)]
    return jnp.concatenate(parts, axis=1).astype(BF16)


def kernel(x, norm_mix_g, norm_ffn_g, final_norm_g, ev_in_w, s5_lam_re, s5_lam_im, s5_log_step, s5_b_re, s5_b_im, s5_c_re, s5_c_im, s5_d, s5_glu_w, s5_glu_b, ssd_conv_w, ssd_conv_b, ssd_dt_bias, ssd_a_log, ssd_d, ssd_norm_g, ev_out_w, od_in_w, ret_norm_g, gla_gate_w, gla_gate_b, gla_norm_g, od_out_w, ffn_gate_w, ffn_up_w, ffn_down_w):
    bsz, seqlen, d = x.shape
    rows = bsz * seqlen
    depth = norm_mix_g.shape[0]
    h = x.reshape(rows, d)
    wg, wu, wd = ffn_gate_w.astype(BF16), ffn_up_w.astype(BF16), ffn_down_w.astype(BF16)
    for layer in range(depth):
        i = layer // 2
        if layer % 2 == 0:
            s5_w = s5_glu_w.shape[1]
            ssd_inner = ssd_norm_g.shape[1]
            conv_dim = ssd_conv_w.shape[2]
            heads = ssd_dt_bias.shape[1]
            w_in = _narrow_last(ev_in_w[i], (s5_w, ssd_inner, conv_dim, heads), narrow=3)
            u, z, xbc, dtp = norm_proj(h, norm_mix_g[layer], w_in, (s5_w, ssd_inner, conv_dim, LANES), (F32,) * 4)
            y_a = s5_mixer(u.reshape(bsz, seqlen, s5_w), s5_lam_re[i], s5_lam_im[i], s5_log_step[i],
                           s5_b_re[i], s5_b_im[i], s5_c_re[i], s5_c_im[i], s5_d[i].reshape(-1),
                           s5_glu_w[i], s5_glu_b[i]).reshape(rows, s5_w)
            y_b = ssd_mixer(z, xbc, dtp, ssd_conv_w[i], ssd_conv_b[i], ssd_dt_bias[i], ssd_a_log[i],
                            ssd_d[i], ssd_norm_g[i], bsz, seqlen)
            w_out = ev_out_w[i].astype(BF16)
        else:
            kw = gla_gate_w.shape[2]
            vw = ret_norm_g.shape[1]
            rank = gla_gate_w.shape[1]
            w_in = _narrow_last(od_in_w[i], (kw, kw, vw, vw, kw, kw, vw, rank, vw), narrow=7)
            q_r, k_r, v_r, g_r, q_g, k_g, v_g, g_g, a_lr = norm_proj(
                h, norm_mix_g[layer], w_in, (kw, kw, vw, vw, kw, kw, vw, vw, LANES), (F32,) * 9)
            y_a = retention_mixer(q_r, k_r, v_r, g_r, ret_norm_g[i], bsz, seqlen)
            y_b = gla_mixer(q_g, k_g, v_g, a_lr, g_g, gla_gate_w[i], gla_gate_b[i], gla_norm_g[i], bsz, seqlen)
            w_out = od_out_w[i].astype(BF16)
        h = mix_ffn(h, y_a, y_b, w_out, norm_ffn_g[layer], wg, wu, wd, layer,
                    final_norm_g, final_norm=(layer == depth - 1))
    return h.reshape(bsz, seqlen, d)
```

```python
import functools
import math

import jax
import jax.numpy as jnp
from jax import lax
from jax.experimental import pallas as pl
from jax.experimental.pallas import tpu as pltpu

F32 = jnp.float32
BF16 = jnp.bfloat16

NORM_EPS = 1e-6
VMEM_LIMIT_BYTES = 56 * 1024 * 1024
LANES = 128
SUBLANES = 8

ROW_TILE = 512
FFN_CHUNK = 256


def _rms(x, g):
    return x * lax.rsqrt(jnp.mean(x * x, axis=-1, keepdims=True) + NORM_EPS) * g


def _dot(a, b):
    return jnp.dot(a, b, preferred_element_type=F32)


def _resident(shape):
    zeros = (0,) * len(shape)
    return pl.BlockSpec(shape, lambda *_: zeros, pipeline_mode=pl.Buffered(1))


def _params(*semantics):
    return pltpu.CompilerParams(dimension_semantics=semantics,
                                vmem_limit_bytes=VMEM_LIMIT_BYTES)


def _mix_ffn_kernel(h_ref, ya_ref, yb_ref, wo_ref, g_ref, wg_ref, wu_ref, wd_ref,
                    ng_ref, *refs, final_norm):
    act_ref = refs[-1]
    na = ya_ref.shape[1]
    h1 = h_ref[...] + _dot(ya_ref[...], wo_ref[:na, :]) + _dot(yb_ref[...], wo_ref[na:, :])
    xn = _rms(h1, g_ref[...]).astype(BF16)
    hidden = wg_ref.shape[1]
    for j in range(hidden // FFN_CHUNK):
        cols = slice(j * FFN_CHUNK, (j + 1) * FFN_CHUNK)
        gate = _dot(xn, wg_ref[:, cols])
        up = _dot(xn, wu_ref[:, cols])
        act_ref[:, cols] = (gate * jax.nn.sigmoid(gate) * up).astype(BF16)
    out = h1 + _dot(act_ref[...], wd_ref[...])
    normed = _rms(out, ng_ref[...])
    if final_norm:
        refs[0][...] = normed
    else:
        refs[0][...] = out
        refs[1][...] = normed.astype(BF16)


def _resident_layer(shape, layer):
    tail = (0,) * (len(shape) - 1)
    return pl.BlockSpec((None,) + tuple(shape[1:]), lambda *_: (layer,) + tail, pipeline_mode=pl.Buffered(1))


def mix_ffn(h, ya, yb, wo, g, wg, wu, wd, layer, next_g, final_norm, row_tile=ROW_TILE):
    rows, d = h.shape
    hidden = wg.shape[2]
    assert rows % row_tile == 0 and hidden % FFN_CHUNK == 0
    assert wo.shape == (ya.shape[1] + yb.shape[1], d)
    row = lambda n: pl.BlockSpec((row_tile, n), lambda i: (i, 0))
    out_shape = [jax.ShapeDtypeStruct((rows, d), F32)]
    if not final_norm:
        out_shape.append(jax.ShapeDtypeStruct((rows, d), BF16))
    outs = pl.pallas_call(
        functools.partial(_mix_ffn_kernel, final_norm=final_norm),
        grid=(rows // row_tile,),
        in_specs=[row(d), row(ya.shape[1]), row(yb.shape[1]), _resident(wo.shape), _resident((1, d)),
                  _resident_layer(wg.shape, layer), _resident_layer(wu.shape, layer),
                  _resident_layer(wd.shape, layer), _resident((1, d))],
        out_specs=[row(d)] * len(out_shape),
        out_shape=out_shape,
        scratch_shapes=[pltpu.VMEM((row_tile, hidden), BF16)],
        compiler_params=_params("parallel"), name="mix_ffn",
    )(h, ya, yb, wo, g.reshape(1, d), wg, wu, wd, next_g.reshape(1, d))
    return outs[0] if final_norm else tuple(outs)


def _split3(a):
    hi = a.astype(BF16)
    r1 = a - hi.astype(F32)
    mid = r1.astype(BF16)
    lo = (r1 - mid.astype(F32)).astype(BF16)
    return hi, mid, lo


def _select_dot(a, sel):
    hi, mid, lo = _split3(a)
    return _dot(hi, sel) + _dot(mid, sel) + _dot(lo, sel)


def _dot_select(sel, a):
    hi, mid, lo = _split3(a)
    return _dot(sel, hi) + _dot(sel, mid) + _dot(sel, lo)


def _dot_nt(a, b):
    return lax.dot_general(a, b, (((1,), (1,)), ((), ())), preferred_element_type=F32)


def _silu(x):
    return x * jax.nn.sigmoid(x)


def _softplus(x):
    return jnp.maximum(x, 0.0) + jnp.log1p(jnp.exp(-jnp.abs(x)))


SSD_TILE = 256
SSD_HEADS = 16
SSD_HEADDIM = 64
SSD_GROUPS = 4
SSD_STATE = 64
SSD_CONV = 4


def _ssd_kernel(h_ref, hg_ref, w_ref, cw_ref, cb_ref, dtb_ref, alog_ref, dsk_ref, ng_ref,
                expand_ref, tril_ref, hmask_ref, gmask_ref, bdmask_ref,
                o_ref, xpad_ref, state_ref, lhs_ref, rhs_ref):
    cs = h_ref.shape[0]
    inner = o_ref.shape[1]
    gw = inner // SSD_GROUPS
    hpg = SSD_HEADS // SSD_GROUPS
    nslab = xpad_ref.shape[0]
    conv_dim = nslab * LANES

    hn = _rms(h_ref[...], hg_ref[...]).astype(BF16)
    z = _dot(hn, w_ref[:, :inner])
    xbc = _dot(hn, w_ref[:, inner:inner + conv_dim])
    dt_raw = _dot(hn, w_ref[:, inner + conv_dim:])

    @pl.when(pl.program_id(1) == 0)
    def _():
        xpad_ref[:, 0:8, :] = jnp.zeros((nslab, 8, LANES), F32)
        state_ref[...] = jnp.zeros(state_ref.shape, F32)

    convs = []
    for j in range(nslab):
        lanes = slice(j * LANES, (j + 1) * LANES)
        xpad_ref[j, 8:8 + cs, :] = xbc[:, lanes]
        conv = cb_ref[:, lanes]
        for k in range(SSD_CONV):
            lo = 8 - (SSD_CONV - 1) + k
            conv = conv + cw_ref[k:k + 1, lanes] * xpad_ref[j, lo:lo + cs, :]
        xpad_ref[j, 0:8, :] = xpad_ref[j, cs:cs + 8, :]
        convs.append(conv)
    xc = _silu(jnp.concatenate(convs, axis=1))
    xs = xc[:, :inner]
    bm = xc[:, inner:inner + SSD_GROUPS * SSD_STATE]
    cm = xc[:, inner + SSD_GROUPS * SSD_STATE:]

    dt = _softplus(dt_raw + dtb_ref[...])
    da = dt * -jnp.exp(alog_ref[...])
    a_cum = _dot_select(tril_ref[...], da)
    a_cum_t = a_cum.T
    expand = expand_ref[...]
    xdt = xs * _select_dot(dt, expand)
    a_x = _select_dot(a_cum, expand)
    a_last = a_x[cs - 1:cs, :]
    decay_in = jnp.exp(a_x)
    xd = (xdt * jnp.exp(a_last - a_x)).astype(BF16)
    xdt = xdt.astype(BF16)

    state = state_ref[...]
    y = _dot(cm.astype(BF16), state.astype(BF16)) * decay_in
    new_state = state * jnp.exp(a_last) + _dot(bm.T.astype(BF16), xd)
    state_ref[...] = new_state * bdmask_ref[...]

    row = lax.broadcasted_iota(jnp.int32, (cs, cs), 0)
    col = lax.broadcasted_iota(jnp.int32, (cs, cs), 1)
    causal = row >= col
    bm16 = bm.astype(BF16)
    ys = []
    for g in range(SSD_GROUPS):
        cb = _dot_nt((cm * gmask_ref[g:g + 1, :]).astype(BF16), bm16)
        xg = xdt[:, g * gw:(g + 1) * gw]
        for j in range(hpg):
            h = g * hpg + j
            diff = a_cum[:, h:h + 1] - a_cum_t[h:h + 1, :]
            seg = jnp.exp(jnp.where(causal, diff, -jnp.inf))
            lhs_ref[:, j * cs:(j + 1) * cs] = (cb * seg).astype(BF16)
            rhs_ref[j * cs:(j + 1) * cs, :] = xg * hmask_ref[j:j + 1, :]
        ys.append(_dot(lhs_ref[...], rhs_ref[...]))
    y = y + jnp.concatenate(ys, axis=1) + dsk_ref[...] * xs
    y = y * _silu(z)
    o_ref[...] = _rms(y, ng_ref[...]).astype(o_ref.dtype)


def ssd_mixer(h, h_norm_g, w_in, conv_w, conv_b, dt_bias, a_log, d_skip, norm_g, batch, seqlen):
    cs = SSD_TILE
    d_model = h.shape[1]
    inner = SSD_HEADS * SSD_HEADDIM
    conv_dim = conv_w.shape[1]
    gw = inner // SSD_GROUPS
    gs = SSD_GROUPS * SSD_STATE
    hpg = SSD_HEADS // SSD_GROUPS
    nt = seqlen // cs
    assert seqlen % cs == 0 and h.shape[0] == batch * seqlen and conv_dim % LANES == 0
    assert w_in.shape == (d_model, inner + conv_dim + LANES)

    pad = lambda v: jnp.pad(v.astype(F32), (0, LANES - SSD_HEADS)).reshape(1, LANES)
    lane_head = jnp.arange(inner) // SSD_HEADDIM
    expand = (jnp.arange(LANES)[:, None] == lane_head[None, :]).astype(BF16)
    tril = (jnp.arange(cs)[:, None] >= jnp.arange(cs)[None, :]).astype(BF16)
    hmask = (jnp.arange(hpg)[:, None] == (jnp.arange(gw) // SSD_HEADDIM)[None, :]).astype(BF16)
    gmask = (jnp.arange(SSD_GROUPS)[:, None] == (jnp.arange(gs) // SSD_STATE)[None, :]).astype(F32)
    bdmask = ((jnp.arange(gs) // SSD_STATE)[:, None] == (jnp.arange(inner) // gw)[None, :]).astype(F32)

    tok = lambda n: pl.BlockSpec((cs, n), lambda b, c: (b * nt + c, 0))
    consts = [h_norm_g.astype(F32).reshape(1, d_model), w_in,
              conv_w.astype(F32), conv_b.astype(F32).reshape(1, -1), pad(dt_bias), pad(a_log),
              jnp.repeat(d_skip.astype(F32), SSD_HEADDIM).reshape(1, inner), norm_g.astype(F32).reshape(1, inner),
              expand, tril, hmask, gmask, bdmask]
    return pl.pallas_call(
        _ssd_kernel, grid=(batch, nt),
        in_specs=[tok(d_model)] + [_resident(c.shape) for c in consts],
        out_specs=tok(inner),
        out_shape=jax.ShapeDtypeStruct((batch * seqlen, inner), BF16),
        scratch_shapes=[pltpu.VMEM((conv_dim // LANES, cs + 8, LANES), F32), pltpu.VMEM((gs, inner), F32),
                        pltpu.VMEM((cs, hpg * cs), BF16), pltpu.VMEM((hpg * cs, gw), BF16)],
        compiler_params=_params("parallel", "arbitrary"), name="ssd_mixer",
    )(h, *consts)


S5_TILE = 64
S5_GROUP = 16
S5_STATE = 64
S5_GROUPS_PER_BLOCK = 16


def _s5_kernel(h_ref, ng_ref, wu_ref, wb_ref, a_ref, wc_ref, d_ref, gw_ref, gb_ref, o_ref, bu_ref, x_ref):
    bsz, tl, d_model = h_ref.shape
    width = wu_ref.shape[1]
    nblk = wb_ref.shape[0]
    kw = wb_ref.shape[1]
    sw = a_ref.shape[2]
    rows = tl * bsz

    @pl.when(pl.program_id(0) == 0)
    def _():
        x_ref[...] = jnp.zeros(x_ref.shape, F32)

    hn = _rms(h_ref[...].reshape(rows, d_model), ng_ref[...]).astype(BF16)
    u = _dot(hn, wu_ref[...])
    u = jnp.swapaxes(u.reshape(bsz, tl, width), 0, 1).reshape(rows, width)
    u16 = u.astype(BF16)
    for kb in range(nblk):
        bu_ref[:, kb * 2 * sw:(kb + 1) * 2 * sw] = _dot(u16[:, kb * kw:(kb + 1) * kw], wb_ref[kb])

    for kb in range(nblk):
        re = slice(kb * 2 * sw, kb * 2 * sw + sw)
        im = slice(kb * 2 * sw + sw, (kb + 1) * 2 * sw)
        a_re = jnp.broadcast_to(a_ref[kb, 0:1, :], (bsz, sw))
        a_im = jnp.broadcast_to(a_ref[kb, 1:2, :], (bsz, sw))

        def step(t, carry, re=re, im=im, a_re=a_re, a_im=a_im):
            x_re, x_im = carry
            r = pl.ds(pl.multiple_of(t * bsz, bsz), bsz)
            n_re = a_re * x_re - a_im * x_im + bu_ref[r, re]
            n_im = a_re * x_im + a_im * x_re + bu_ref[r, im]
            bu_ref[r, re] = n_re
            bu_ref[r, im] = n_im
            return n_re, n_im

        x_re, x_im = lax.fori_loop(0, tl, step, (x_ref[:, re], x_ref[:, im]), unroll=4)
        x_ref[:, re] = x_re
        x_ref[:, im] = x_im

    ys = []
    for kb in range(nblk):
        xs16 = bu_ref[:, kb * 2 * sw:(kb + 1) * 2 * sw].astype(BF16)
        ys.append(_dot(xs16, wc_ref[kb]))
    y = jnp.concatenate(ys, axis=1) + d_ref[...] * u
    z = jax.nn.gelu(y)
    out = z * jax.nn.sigmoid(_dot(z.astype(BF16), gw_ref[...]) + gb_ref[...])
    o_ref[...] = jnp.swapaxes(out.reshape(tl, bsz, width), 0, 1).astype(o_ref.dtype)


def _s5_weights(lam_re, lam_im, log_step, b_re, b_im, c_re, c_im):
    lr, li = lam_re.astype(F32), lam_im.astype(F32)
    step = jnp.exp(log_step.astype(F32))[:, None]
    mag = jnp.exp(lr * step)
    a_re, a_im = mag * jnp.cos(li * step), mag * jnp.sin(li * step)
    den = lr * lr + li * li
    k_re = ((a_re - 1.0) * lr + a_im * li) / den
    k_im = (a_im * lr - (a_re - 1.0) * li) / den
    br, bi = b_re.astype(F32), b_im.astype(F32)
    bb_re = k_re[..., None] * br - k_im[..., None] * bi
    bb_im = k_re[..., None] * bi + k_im[..., None] * br
    groups, nstate, nch = br.shape
    gpb = S5_GROUPS_PER_BLOCK
    nblk = groups // gpb
    eye = jnp.eye(gpb, dtype=F32)

    def in_block(bb):
        t = bb.reshape(nblk, gpb, nstate, nch).transpose(0, 1, 3, 2)
        t = t[:, :, :, None, :] * eye[None, :, None, :, None]
        return t.reshape(nblk, gpb * nch, gpb * nstate)

    def out_block(c):
        t = c.astype(F32).reshape(nblk, gpb, nch, nstate).transpose(0, 1, 3, 2)
        t = t[:, :, :, None, :] * eye[None, :, None, :, None]
        return t.reshape(nblk, gpb * nstate, gpb * nch)

    wb = jnp.concatenate([in_block(bb_re), in_block(bb_im)], axis=2).astype(BF16)
    wc = jnp.concatenate([out_block(c_re), -out_block(c_im)], axis=1).astype(BF16)
    a = jnp.stack([a_re.reshape(nblk, gpb * nstate), a_im.reshape(nblk, gpb * nstate)], axis=1)
    return wb, a, wc


def s5_mixer(h, norm_g, w_u, lam_re, lam_im, log_step, b_re, b_im, c_re, c_im, d_skip, glu_w, glu_b):
    bsz, seqlen, d_model = h.shape
    width = w_u.shape[1]
    tl = S5_TILE
    assert bsz == SUBLANES and seqlen % tl == 0
    wb, a, wc = _s5_weights(lam_re, lam_im, log_step, b_re, b_im, c_re, c_im)
    nstates = wb.shape[0] * wb.shape[2]
    consts = [norm_g.astype(F32).reshape(1, d_model), w_u, wb, a, wc, d_skip.astype(F32).reshape(1, width),
              glu_w.astype(BF16), glu_b.astype(F32).reshape(1, width)]
    tok = lambda n: pl.BlockSpec((bsz, tl, n), lambda i: (0, i, 0))
    return pl.pallas_call(
        _s5_kernel, grid=(seqlen // tl,),
        in_specs=[tok(d_model)] + [_resident(c.shape) for c in consts],
        out_specs=tok(width),
        out_shape=jax.ShapeDtypeStruct((bsz, seqlen, width), BF16),
        scratch_shapes=[pltpu.VMEM((tl * bsz, nstates), F32), pltpu.VMEM((bsz, nstates), F32)],
        compiler_params=_params("arbitrary"), name="s5_mixer",
    )(h, *consts)


MIX_TILE = 256
MIX_HEADS = 4
MIX_QK = 128
MIX_V = 256
ROPE_BASE = 10000.0


def _dot_tn(a, b):
    return lax.dot_general(a, b, (((0,), (0,)), ((), ())), preferred_element_type=F32)


def _rotary(t, cos, sin_signed):
    return t * cos + pltpu.roll(t, MIX_QK // 2, axis=1) * sin_signed


def _ret_kernel(hn_ref, w_ref, cos_ref, sin_ref, dmat_ref, qdec_ref, kdec_ref, cdec_ref,
                ng_ref, o_ref, state_ref):
    @pl.when(pl.program_id(1) == 0)
    def _():
        state_ref[...] = jnp.zeros(state_ref.shape, F32)

    kw, vw = MIX_HEADS * MIX_QK, MIX_HEADS * MIX_V
    hn = hn_ref[...]
    qk = _dot(hn, w_ref[:, :2 * kw])
    v_all = _dot(hn, w_ref[:, 2 * kw:2 * kw + vw]).astype(BF16)
    g_all = _dot(hn, w_ref[:, 2 * kw + vw:])
    cos, sin = cos_ref[...], sin_ref[...]
    scale = MIX_QK ** -0.5
    for h in range(MIX_HEADS):
        vl = slice(h * MIX_V, (h + 1) * MIX_V)
        q = _rotary(qk[:, h * MIX_QK:(h + 1) * MIX_QK], cos, sin)
        k = _rotary(qk[:, kw + h * MIX_QK:kw + (h + 1) * MIX_QK], cos, sin) * scale
        v16 = v_all[:, vl]
        q16 = q.astype(BF16)
        inner = _dot_nt(q16, k.astype(BF16)) * dmat_ref[h]
        state = state_ref[h]
        out = _dot(inner.astype(BF16), v16) + _dot_nt((q * qdec_ref[h]).astype(BF16), state.astype(BF16))
        state_ref[h] = state * cdec_ref[h] + _dot_tn(v16, (k * kdec_ref[h]).astype(BF16))
        mu = jnp.mean(out, axis=-1, keepdims=True)
        cen = out - mu
        var = jnp.mean(cen * cen, axis=-1, keepdims=True)
        o = cen * lax.rsqrt(var + NORM_EPS) * ng_ref[:, vl]
        o_ref[:, vl] = (o * _silu(g_all[:, vl])).astype(o_ref.dtype)


def _rope_tables(seqlen):
    half = MIX_QK // 2
    inv_freq = ROPE_BASE ** (-jnp.arange(half, dtype=F32) / half)
    ang = jnp.arange(seqlen, dtype=F32)[:, None] * inv_freq[None, :]
    cos, sin = jnp.cos(ang), jnp.sin(ang)
    return jnp.concatenate([cos, cos], axis=1), jnp.concatenate([-sin, sin], axis=1)


def retention_mixer(hn, w_in, norm_g, batch, seqlen):
    c = MIX_TILE
    nt = seqlen // c
    d_model = hn.shape[1]
    kw, vw = MIX_HEADS * MIX_QK, MIX_HEADS * MIX_V
    assert seqlen % c == 0 and hn.shape[0] == batch * seqlen and w_in.shape == (d_model, 2 * kw + 2 * vw)
    cos, sin = _rope_tables(seqlen)
    log_gamma = jnp.log(1.0 - 2.0 ** (-5.0 - jnp.arange(MIX_HEADS, dtype=F32)))
    pos = jnp.arange(c, dtype=F32)
    diff = pos[:, None] - pos[None, :]
    dmat = jnp.where(diff >= 0, jnp.exp(log_gamma[:, None, None] * jnp.maximum(diff, 0.0)), 0.0)
    lanes = lambda col: jnp.broadcast_to(col[:, :, None], (MIX_HEADS, c, MIX_QK))
    qdec = lanes(jnp.exp(log_gamma[:, None] * (pos + 1.0)))
    kdec = lanes(jnp.exp(log_gamma[:, None] * (c - 1.0 - pos)))
    cdec = jnp.broadcast_to(jnp.exp(log_gamma * c)[:, None, None], (MIX_HEADS, 1, MIX_QK))
    consts = [dmat, qdec, kdec, cdec, norm_g.astype(F32).reshape(1, vw)]
    tok = lambda n: pl.BlockSpec((c, n), lambda b, i: (b * nt + i, 0))
    rope = pl.BlockSpec((c, MIX_QK), lambda b, i: (i, 0))
    return pl.pallas_call(
        _ret_kernel, grid=(batch, nt),
        in_specs=[tok(d_model), _resident(w_in.shape), rope, rope] + [_resident(x.shape) for x in consts],
        out_specs=tok(vw),
        out_shape=jax.ShapeDtypeStruct((batch * seqlen, vw), BF16),
        scratch_shapes=[pltpu.VMEM((MIX_HEADS, MIX_V, MIX_QK), F32)],
        compiler_params=_params("parallel", "arbitrary"), name="retention_mixer",
    )(hn, w_in, cos, sin, *consts)


GLA_CHUNK = 64
GLA_TAU = 16.0


def _log_sigmoid(x):
    return jnp.minimum(x, 0.0) - jnp.log1p(jnp.exp(-jnp.abs(x)))


def _gla_kernel(hn_ref, w_ref, gw_ref, gb_ref, ng_ref, cumsel_ref, lastsel_ref,
                o_ref, state_ref):
    tl = hn_ref.shape[0]
    nchunk = tl // GLA_CHUNK
    kw, vw = MIX_HEADS * MIX_QK, MIX_HEADS * MIX_V

    @pl.when(pl.program_id(1) == 0)
    def _():
        state_ref[...] = jnp.zeros(state_ref.shape, F32)

    hn = hn_ref[...]
    qk = _dot(hn, w_ref[:, :2 * kw])
    v_all = _dot(hn, w_ref[:, 2 * kw:2 * kw + vw])
    g_all = _dot(hn, w_ref[:, 2 * kw + vw:2 * kw + 2 * vw])
    a_lr = _dot(hn, w_ref[:, 2 * kw + 2 * vw:])
    log_a = _log_sigmoid(_dot(a_lr.astype(BF16), gw_ref[...]) + gb_ref[...]) / GLA_TAU
    bcum = _dot_select(cumsel_ref[...], log_a)
    blast = _dot_select(lastsel_ref[...], log_a)
    grow = jnp.exp(bcum)
    row = lax.broadcasted_iota(jnp.int32, (tl, tl), 0)
    col = lax.broadcasted_iota(jnp.int32, (tl, tl), 1)
    shift = GLA_CHUNK.bit_length() - 1
    keep = (row >= col) & (jnp.right_shift(row, shift) == jnp.right_shift(col, shift))
    col_chunk = jnp.right_shift(lax.broadcasted_iota(jnp.int32, (1, tl), 1), shift)
    scale = MIX_QK ** -0.5
    for h in range(MIX_HEADS):
        kl = slice(h * MIX_QK, (h + 1) * MIX_QK)
        vl = slice(h * MIX_V, (h + 1) * MIX_V)
        k = qk[:, kw + h * MIX_QK:kw + (h + 1) * MIX_QK]
        q_in = (qk[:, kl] * scale * grow[:, kl]).astype(BF16)
        k_in = (k * jnp.exp(-bcum[:, kl])).astype(BF16)
        k_out = (k * jnp.exp(blast[:, kl] - bcum[:, kl])).astype(BF16)
        v = v_all[:, vl]
        v16 = v.astype(BF16)
        att = jnp.where(keep, _dot_nt(q_in, k_in), 0.0)
        out = _dot(att.astype(BF16), v16)
        v_t = v.T
        stacked = jnp.concatenate(
            [jnp.where(col_chunk == c, v_t, 0.0) for c in range(nchunk)], axis=0).astype(BF16)
        kv = _dot(stacked, k_out)
        state = state_ref[h]
        inter = []
        for c in range(nchunk):
            rows = slice(c * GLA_CHUNK, (c + 1) * GLA_CHUNK)
            inter.append(_dot_nt(q_in[rows, :], state.astype(BF16)))
            decay = jnp.exp(blast[c * GLA_CHUNK:c * GLA_CHUNK + 1, kl])
            state = state * decay + kv[c * MIX_V:(c + 1) * MIX_V, :]
        state_ref[h] = state
        out = out + jnp.concatenate(inter, axis=0)
        o = out * lax.rsqrt(jnp.mean(out * out, axis=-1, keepdims=True) + NORM_EPS) * ng_ref[:, vl]
        o_ref[:, vl] = (o * _silu(g_all[:, vl])).astype(o_ref.dtype)


def gla_mixer(hn, w_in, gate_w, gate_b, norm_g, batch, seqlen):
    tl = MIX_TILE
    nt = seqlen // tl
    d_model = hn.shape[1]
    kw, vw = MIX_HEADS * MIX_QK, MIX_HEADS * MIX_V
    assert seqlen % tl == 0 and tl % GLA_CHUNK == 0 and hn.shape[0] == batch * seqlen
    assert w_in.shape == (d_model, 2 * kw + 2 * vw + LANES)
    rank = gate_w.shape[0]
    gw = jnp.pad(gate_w.astype(F32), ((0, LANES - rank), (0, 0))).astype(BF16)
    t = jnp.arange(tl)
    same = (t[:, None] // GLA_CHUNK) == (t[None, :] // GLA_CHUNK)
    cumsel = (same & (t[:, None] >= t[None, :])).astype(BF16)
    lastsel = same.astype(BF16)
    consts = [gw, gate_b.astype(F32).reshape(1, kw), norm_g.astype(F32).reshape(1, vw), cumsel, lastsel]
    tok = lambda n: pl.BlockSpec((tl, n), lambda b, i: (b * nt + i, 0))
    return pl.pallas_call(
        _gla_kernel, grid=(batch, nt),
        in_specs=[tok(d_model), _resident(w_in.shape)] + [_resident(x.shape) for x in consts],
        out_specs=tok(vw),
        out_shape=jax.ShapeDtypeStruct((batch * seqlen, vw), BF16),
        scratch_shapes=[pltpu.VMEM((MIX_HEADS, MIX_V, MIX_QK), F32)],
        compiler_params=_params("parallel", "arbitrary"), name="gla_mixer",
    )(hn, w_in, *consts)


def _slabs_narrow_last(w, lo, hi, narrow_lo, narrow_hi):
    n = narrow_hi - narrow_lo
    assert lo <= narrow_lo < narrow_hi <= hi and n < LANES
    parts = [w[:, lo:narrow_lo], w[:, narrow_hi:hi], w[:, narrow_lo:narrow_hi],
             jnp.zeros((w.shape[0], LANES - n), w.dtype)]
    return jnp.concatenate(parts, axis=1).astype(BF16)


def kernel(x, norm_mix_g, norm_ffn_g, final_norm_g, ev_in_w, s5_lam_re, s5_lam_im, s5_log_step, s5_b_re, s5_b_im, s5_c_re, s5_c_im, s5_d, s5_glu_w, s5_glu_b, ssd_conv_w, ssd_conv_b, ssd_dt_bias, ssd_a_log, ssd_d, ssd_norm_g, ev_out_w, od_in_w, ret_norm_g, gla_gate_w, gla_gate_b, gla_norm_g, od_out_w, ffn_gate_w, ffn_up_w, ffn_down_w):
    bsz, seqlen, d = x.shape
    rows = bsz * seqlen
    depth = norm_mix_g.shape[0]
    h = x.reshape(rows, d)
    hn = None
    wg, wu, wd = ffn_gate_w.astype(BF16), ffn_up_w.astype(BF16), ffn_down_w.astype(BF16)
    for layer in range(depth):
        i = layer // 2
        if layer % 2 == 0:
            s5_w = s5_glu_w.shape[1]
            w = ev_in_w[i]
            total = w.shape[1]
            heads = ssd_dt_bias.shape[1]
            w_s5 = w[:, :s5_w].astype(BF16)
            w_ssd = _slabs_narrow_last(w, s5_w, total, total - heads, total)
            y_a = s5_mixer(h.reshape(bsz, seqlen, d), norm_mix_g[layer], w_s5, s5_lam_re[i], s5_lam_im[i],
                           s5_log_step[i], s5_b_re[i], s5_b_im[i], s5_c_re[i], s5_c_im[i], s5_d[i].reshape(-1),
                           s5_glu_w[i], s5_glu_b[i]).reshape(rows, s5_w)
            y_b = ssd_mixer(h, norm_mix_g[layer], w_ssd, ssd_conv_w[i], ssd_conv_b[i], ssd_dt_bias[i],
                            ssd_a_log[i], ssd_d[i], ssd_norm_g[i], bsz, seqlen)
            w_out = ev_out_w[i].astype(BF16)
        else:
            kw = gla_gate_w.shape[2]
            vw = ret_norm_g.shape[1]
            rank = gla_gate_w.shape[1]
            w = od_in_w[i]
            ret_w = 2 * kw + 2 * vw
            w_ret = w[:, :ret_w].astype(BF16)
            a_lo = ret_w + 2 * kw + vw
            w_gla = _slabs_narrow_last(w, ret_w, w.shape[1], a_lo, a_lo + rank)
            assert hn is not None, "odd layers follow an even layer"
            y_a = retention_mixer(hn, w_ret, ret_norm_g[i], bsz, seqlen)
            y_b = gla_mixer(hn, w_gla, gla_gate_w[i], gla_gate_b[i], gla_norm_g[i], bsz, seqlen)
            w_out = od_out_w[i].astype(BF16)
        if layer == depth - 1:
            h = mix_ffn(h, y_a, y_b, w_out, norm_ffn_g[layer], wg, wu, wd, layer, final_norm_g, final_norm=True)
        else:
            h, hn = mix_ffn(h, y_a, y_b, w_out, norm_ffn_g[layer], wg, wu, wd, layer,
                            norm_mix_g[layer + 1], final_norm=False)
    return h.reshape(bsz, seqlen, d)
```

```python
import functools
import math

import jax
import jax.numpy as jnp
from jax import lax
from jax.experimental import pallas as pl
from jax.experimental.pallas import tpu as pltpu

F32 = jnp.float32
BF16 = jnp.bfloat16

NORM_EPS = 1e-6
VMEM_LIMIT_BYTES = 56 * 1024 * 1024
LANES = 128
SUBLANES = 8

ROW_TILE = 512
FFN_CHUNK = 256


def _rms(x, g):
    return x * lax.rsqrt(jnp.mean(x * x, axis=-1, keepdims=True) + NORM_EPS) * g


def _dot(a, b):
    return jnp.dot(a, b, preferred_element_type=F32)


def _resident(shape):
    zeros = (0,) * len(shape)
    return pl.BlockSpec(shape, lambda *_: zeros, pipeline_mode=pl.Buffered(1))


def _params(*semantics):
    return pltpu.CompilerParams(dimension_semantics=semantics,
                                vmem_limit_bytes=VMEM_LIMIT_BYTES)


def _mix_ffn_kernel(h_ref, ya_ref, yb_ref, wo_ref, g_ref, wg_ref, wu_ref, wd_ref,
                    ng_ref, *refs, final_norm):
    act_ref = refs[-1]
    na = ya_ref.shape[1]
    h1 = h_ref[...] + _dot(ya_ref[...], wo_ref[:na, :]) + _dot(yb_ref[...], wo_ref[na:, :])
    xn = _rms(h1, g_ref[...]).astype(BF16)
    hidden = wg_ref.shape[1]
    for j in range(hidden // FFN_CHUNK):
        cols = slice(j * FFN_CHUNK, (j + 1) * FFN_CHUNK)
        gate = _dot(xn, wg_ref[:, cols])
        up = _dot(xn, wu_ref[:, cols])
        act_ref[:, cols] = (gate * jax.nn.sigmoid(gate) * up).astype(BF16)
    out = h1 + _dot(act_ref[...], wd_ref[...])
    normed = _rms(out, ng_ref[...])
    if final_norm:
        refs[0][...] = normed
    else:
        refs[0][...] = out
        refs[1][...] = normed.astype(BF16)


def _resident_layer(shape, layer):
    tail = (0,) * (len(shape) - 1)
    return pl.BlockSpec((None,) + tuple(shape[1:]), lambda *_: (layer,) + tail, pipeline_mode=pl.Buffered(1))


def mix_ffn(h, ya, yb, wo, g, wg, wu, wd, layer, next_g, final_norm, row_tile=ROW_TILE):
    rows, d = h.shape
    hidden = wg.shape[2]
    assert rows % row_tile == 0 and hidden % FFN_CHUNK == 0
    assert wo.shape == (ya.shape[1] + yb.shape[1], d)
    row = lambda n: pl.BlockSpec((row_tile, n), lambda i: (i, 0))
    out_shape = [jax.ShapeDtypeStruct((rows, d), F32)]
    if not final_norm:
        out_shape.append(jax.ShapeDtypeStruct((rows, d), BF16))
    outs = pl.pallas_call(
        functools.partial(_mix_ffn_kernel, final_norm=final_norm),
        grid=(rows // row_tile,),
        in_specs=[row(d), row(ya.shape[1]), row(yb.shape[1]), _resident(wo.shape), _resident((1, d)),
                  _resident_layer(wg.shape, layer), _resident_layer(wu.shape, layer),
                  _resident_layer(wd.shape, layer), _resident((1, d))],
        out_specs=[row(d)] * len(out_shape),
        out_shape=out_shape,
        scratch_shapes=[pltpu.VMEM((row_tile, hidden), BF16)],
        compiler_params=_params("parallel"), name="mix_ffn",
    )(h, ya, yb, wo, g.reshape(1, d), wg, wu, wd, next_g.reshape(1, d))
    return outs[0] if final_norm else tuple(outs)


def _split3(a):
    hi = a.astype(BF16)
    r1 = a - hi.astype(F32)
    mid = r1.astype(BF16)
    lo = (r1 - mid.astype(F32)).astype(BF16)
    return hi, mid, lo


def _select_dot(a, sel):
    hi, mid, lo = _split3(a)
    return _dot(hi, sel) + _dot(mid, sel) + _dot(lo, sel)


def _dot_select(sel, a):
    hi, mid, lo = _split3(a)
    return _dot(sel, hi) + _dot(sel, mid) + _dot(sel, lo)


def _dot_nt(a, b):
    return lax.dot_general(a, b, (((1,), (1,)), ((), ())), preferred_element_type=F32)


def _silu(x):
    return x * jax.nn.sigmoid(x)


def _softplus(x):
    return jnp.maximum(x, 0.0) + jnp.log1p(jnp.exp(-jnp.abs(x)))


SSD_TILE = 256
SSD_HEADS = 16
SSD_HEADDIM = 64
SSD_GROUPS = 4
SSD_STATE = 64
SSD_CONV = 4


def _ssd_kernel(h_ref, hg_ref, w_ref, cw_ref, cb_ref, dtb_ref, alog_ref, dsk_ref, ng_ref,
                expand_ref, tril_ref, hmask_ref, gmask_ref, bdmask_ref,
                o_ref, xpad_ref, state_ref, lhs_ref, rhs_ref):
    cs = h_ref.shape[0]
    inner = o_ref.shape[1]
    gw = inner // SSD_GROUPS
    hpg = SSD_HEADS // SSD_GROUPS
    nslab = xpad_ref.shape[0]
    conv_dim = nslab * LANES

    hn = _rms(h_ref[...], hg_ref[...]).astype(BF16)
    z = _dot(hn, w_ref[:, :inner])
    xbc = _dot(hn, w_ref[:, inner:inner + conv_dim])
    dt_raw = _dot(hn, w_ref[:, inner + conv_dim:])

    @pl.when(pl.program_id(1) == 0)
    def _():
        xpad_ref[:, 0:8, :] = jnp.zeros((nslab, 8, LANES), F32)
        state_ref[...] = jnp.zeros(state_ref.shape, F32)

    convs = []
    for j in range(nslab):
        lanes = slice(j * LANES, (j + 1) * LANES)
        xpad_ref[j, 8:8 + cs, :] = xbc[:, lanes]
        conv = cb_ref[:, lanes]
        for k in range(SSD_CONV):
            lo = 8 - (SSD_CONV - 1) + k
            conv = conv + cw_ref[k:k + 1, lanes] * xpad_ref[j, lo:lo + cs, :]
        xpad_ref[j, 0:8, :] = xpad_ref[j, cs:cs + 8, :]
        convs.append(conv)
    xc = _silu(jnp.concatenate(convs, axis=1))
    xs = xc[:, :inner]
    bm = xc[:, inner:inner + SSD_GROUPS * SSD_STATE]
    cm = xc[:, inner + SSD_GROUPS * SSD_STATE:]

    dt = _softplus(dt_raw + dtb_ref[...])
    da = dt * -jnp.exp(alog_ref[...])
    a_cum = _dot_select(tril_ref[...], da)
    a_cum_t = a_cum.T
    expand = expand_ref[...]
    xdt = xs * _select_dot(dt, expand)
    a_x = _select_dot(a_cum, expand)
    a_last = a_x[cs - 1:cs, :]
    decay_in = jnp.exp(a_x)
    xd = (xdt * jnp.exp(a_last - a_x)).astype(BF16)
    xdt = xdt.astype(BF16)

    state = state_ref[...]
    y = _dot(cm.astype(BF16), state.astype(BF16)) * decay_in
    new_state = state * jnp.exp(a_last) + _dot(bm.T.astype(BF16), xd)
    state_ref[...] = new_state * bdmask_ref[...]

    row = lax.broadcasted_iota(jnp.int32, (cs, cs), 0)
    col = lax.broadcasted_iota(jnp.int32, (cs, cs), 1)
    causal = row >= col
    bm16 = bm.astype(BF16)
    ys = []
    for g in range(SSD_GROUPS):
        cb = _dot_nt((cm * gmask_ref[g:g + 1, :]).astype(BF16), bm16)
        xg = xdt[:, g * gw:(g + 1) * gw]
        for j in range(hpg):
            h = g * hpg + j
            diff = a_cum[:, h:h + 1] - a_cum_t[h:h + 1, :]
            seg = jnp.exp(jnp.where(causal, diff, -jnp.inf))
            lhs_ref[:, j * cs:(j + 1) * cs] = (cb * seg).astype(BF16)
            rhs_ref[j * cs:(j + 1) * cs, :] = xg * hmask_ref[j:j + 1, :]
        ys.append(_dot(lhs_ref[...], rhs_ref[...]))
    y = y + jnp.concatenate(ys, axis=1) + dsk_ref[...] * xs
    y = y * _silu(z)
    o_ref[...] = _rms(y, ng_ref[...]).astype(o_ref.dtype)


def ssd_mixer(h, h_norm_g, w_in, conv_w, conv_b, dt_bias, a_log, d_skip, norm_g, batch, seqlen):
    cs = SSD_TILE
    d_model = h.shape[1]
    inner = SSD_HEADS * SSD_HEADDIM
    conv_dim = conv_w.shape[1]
    gw = inner // SSD_GROUPS
    gs = SSD_GROUPS * SSD_STATE
    hpg = SSD_HEADS // SSD_GROUPS
    nt = seqlen // cs
    assert seqlen % cs == 0 and h.shape[0] == batch * seqlen and conv_dim % LANES == 0
    assert w_in.shape == (d_model, inner + conv_dim + LANES)

    pad = lambda v: jnp.pad(v.astype(F32), (0, LANES - SSD_HEADS)).reshape(1, LANES)
    lane_head = jnp.arange(inner) // SSD_HEADDIM
    expand = (jnp.arange(LANES)[:, None] == lane_head[None, :]).astype(BF16)
    tril = (jnp.arange(cs)[:, None] >= jnp.arange(cs)[None, :]).astype(BF16)
    hmask = (jnp.arange(hpg)[:, None] == (jnp.arange(gw) // SSD_HEADDIM)[None, :]).astype(BF16)
    gmask = (jnp.arange(SSD_GROUPS)[:, None] == (jnp.arange(gs) // SSD_STATE)[None, :]).astype(F32)
    bdmask = ((jnp.arange(gs) // SSD_STATE)[:, None] == (jnp.arange(inner) // gw)[None, :]).astype(F32)

    tok = lambda n: pl.BlockSpec((cs, n), lambda b, c: (b * nt + c, 0))
    consts = [h_norm_g.astype(F32).reshape(1, d_model), w_in,
              conv_w.astype(F32), conv_b.astype(F32).reshape(1, -1), pad(dt_bias), pad(a_log),
              jnp.repeat(d_skip.astype(F32), SSD_HEADDIM).reshape(1, inner), norm_g.astype(F32).reshape(1, inner),
              expand, tril, hmask, gmask, bdmask]
    return pl.pallas_call(
        _ssd_kernel, grid=(batch, nt),
        in_specs=[tok(d_model)] + [_resident(c.shape) for c in consts],
        out_specs=tok(inner),
        out_shape=jax.ShapeDtypeStruct((batch * seqlen, inner), BF16),
        scratch_shapes=[pltpu.VMEM((conv_dim // LANES, cs + 8, LANES), F32), pltpu.VMEM((gs, inner), F32),
                        pltpu.VMEM((cs, hpg * cs), BF16), pltpu.VMEM((hpg * cs, gw), BF16)],
        compiler_params=_params("parallel", "arbitrary"), name="ssd_mixer",
    )(h, *consts)


S5_TILE = 128
S5_CHUNK = 32
S5_GROUP = 16
S5_STATE = 64
S5_GROUPS_PER_BLOCK = 16


def _s5_kernel(h_ref, ng_ref, wu_ref, wb_ref, a_ref, wc_ref, d_ref, gw_ref, gb_ref, o_ref,
               bu_ref, x_ref, y_ref, abc_ref):
    bsz, tl, d_model = h_ref.shape
    width = wu_ref.shape[1]
    nblk = wb_ref.shape[0]
    kw = wb_ref.shape[1]
    sw = a_ref.shape[2]
    rows = tl * bsz

    lanes = lambda kb: slice(kb * 2 * sw, (kb + 1) * 2 * sw)
    re = lambda kb: slice(kb * 2 * sw, kb * 2 * sw + sw)
    im = lambda kb: slice(kb * 2 * sw + sw, (kb + 1) * 2 * sw)

    @pl.when(pl.program_id(0) == 0)
    def _():
        x_ref[...] = jnp.zeros(x_ref.shape, F32)
        for kb in range(nblk):
            abc_ref[:, re(kb)] = jnp.broadcast_to(a_ref[kb, 0:1, :], (bsz, sw))
            abc_ref[:, im(kb)] = jnp.broadcast_to(a_ref[kb, 1:2, :], (bsz, sw))

    hn = _rms(h_ref[...].reshape(rows, d_model), ng_ref[...]).astype(BF16)
    u = _dot(hn, wu_ref[...])
    u = jnp.swapaxes(u.reshape(bsz, tl, width), 0, 1).reshape(rows, width)
    u16 = u.astype(BF16)

    crows = S5_CHUNK * bsz
    nchunk = tl // S5_CHUNK

    def input_stage(j):
        rows_j = slice(j * crows, (j + 1) * crows)
        for kb in range(nblk):
            bu_ref[rows_j, lanes(kb)] = _dot(u16[rows_j, kb * kw:(kb + 1) * kw], wb_ref[kb])

    def scan_stage(j, state):
        for t in range(j * S5_CHUNK, (j + 1) * S5_CHUNK):
            r = slice(t * bsz, (t + 1) * bsz)
            new_state = []
            for kb in range(nblk):
                x_re, x_im = state[kb]
                a_re, a_im = abc_ref[:, re(kb)], abc_ref[:, im(kb)]
                n_re = a_re * x_re - a_im * x_im + bu_ref[r, re(kb)]
                n_im = a_re * x_im + a_im * x_re + bu_ref[r, im(kb)]
                bu_ref[r, re(kb)] = n_re
                bu_ref[r, im(kb)] = n_im
                new_state.append((n_re, n_im))
            state = new_state
        return state

    def output_stage(j):
        rows_j = slice(j * crows, (j + 1) * crows)
        ys = [_dot(bu_ref[rows_j, lanes(kb)].astype(BF16), wc_ref[kb]) for kb in range(nblk)]
        y = jnp.concatenate(ys, axis=1) + d_ref[...] * u[rows_j, :]
        z = jax.nn.gelu(y)
        y_ref[rows_j, :] = z * jax.nn.sigmoid(_dot(z.astype(BF16), gw_ref[...]) + gb_ref[...])

    state = [(x_ref[:, re(kb)], x_ref[:, im(kb)]) for kb in range(nblk)]
    input_stage(0)
    for j in range(nchunk):
        if j + 1 < nchunk:
            input_stage(j + 1)
        state = scan_stage(j, state)
        output_stage(j)
    for kb in range(nblk):
        x_ref[:, re(kb)] = state[kb][0]
        x_ref[:, im(kb)] = state[kb][1]
    o_ref[...] = jnp.swapaxes(y_ref[...].reshape(tl, bsz, width), 0, 1).astype(o_ref.dtype)


def _s5_weights(lam_re, lam_im, log_step, b_re, b_im, c_re, c_im):
    lr, li = lam_re.astype(F32), lam_im.astype(F32)
    step = jnp.exp(log_step.astype(F32))[:, None]
    mag = jnp.exp(lr * step)
    a_re, a_im = mag * jnp.cos(li * step), mag * jnp.sin(li * step)
    den = lr * lr + li * li
    k_re = ((a_re - 1.0) * lr + a_im * li) / den
    k_im = (a_im * lr - (a_re - 1.0) * li) / den
    br, bi = b_re.astype(F32), b_im.astype(F32)
    bb_re = k_re[..., None] * br - k_im[..., None] * bi
    bb_im = k_re[..., None] * bi + k_im[..., None] * br
    groups, nstate, nch = br.shape
    gpb = S5_GROUPS_PER_BLOCK
    nblk = groups // gpb
    eye = jnp.eye(gpb, dtype=F32)

    def in_block(bb):
        t = bb.reshape(nblk, gpb, nstate, nch).transpose(0, 1, 3, 2)
        t = t[:, :, :, None, :] * eye[None, :, None, :, None]
        return t.reshape(nblk, gpb * nch, gpb * nstate)

    def out_block(c):
        t = c.astype(F32).reshape(nblk, gpb, nch, nstate).transpose(0, 1, 3, 2)
        t = t[:, :, :, None, :] * eye[None, :, None, :, None]
        return t.reshape(nblk, gpb * nstate, gpb * nch)

    wb = jnp.concatenate([in_block(bb_re), in_block(bb_im)], axis=2).astype(BF16)
    wc = jnp.concatenate([out_block(c_re), -out_block(c_im)], axis=1).astype(BF16)
    a = jnp.stack([a_re.reshape(nblk, gpb * nstate), a_im.reshape(nblk, gpb * nstate)], axis=1)
    return wb, a, wc


def s5_mixer(h, norm_g, w_u, lam_re, lam_im, log_step, b_re, b_im, c_re, c_im, d_skip, glu_w, glu_b):
    bsz, seqlen, d_model = h.shape
    width = w_u.shape[1]
    tl = S5_TILE
    assert bsz == SUBLANES and seqlen % tl == 0
    wb, a, wc = _s5_weights(lam_re, lam_im, log_step, b_re, b_im, c_re, c_im)
    nstates = wb.shape[0] * wb.shape[2]
    consts = [norm_g.astype(F32).reshape(1, d_model), w_u, wb, a, wc, d_skip.astype(F32).reshape(1, width),
              glu_w.astype(BF16), glu_b.astype(F32).reshape(1, width)]
    tok = lambda n: pl.BlockSpec((bsz, tl, n), lambda i: (0, i, 0))
    return pl.pallas_call(
        _s5_kernel, grid=(seqlen // tl,),
        in_specs=[tok(d_model)] + [_resident(c.shape) for c in consts],
        out_specs=tok(width),
        out_shape=jax.ShapeDtypeStruct((bsz, seqlen, width), BF16),
        scratch_shapes=[pltpu.VMEM((tl * bsz, nstates), F32), pltpu.VMEM((bsz, nstates), F32),
                        pltpu.VMEM((tl * bsz, width), F32), pltpu.VMEM((bsz, nstates), F32)],
        compiler_params=_params("arbitrary"), name="s5_mixer",
    )(h, *consts)


MIX_TILE = 256
MIX_HEADS = 4
MIX_QK = 128
MIX_V = 256
ROPE_BASE = 10000.0


def _dot_tn(a, b):
    return lax.dot_general(a, b, (((0,), (0,)), ((), ())), preferred_element_type=F32)


def _rotary(t, cos, sin_signed):
    return t * cos + pltpu.roll(t, MIX_QK // 2, axis=1) * sin_signed


def _ret_kernel(hn_ref, w_ref, cos_ref, sin_ref, dmat_ref, qdec_ref, kdec_ref, cdec_ref,
                ng_ref, o_ref, state_ref):
    @pl.when(pl.program_id(1) == 0)
    def _():
        state_ref[...] = jnp.zeros(state_ref.shape, F32)

    kw, vw = MIX_HEADS * MIX_QK, MIX_HEADS * MIX_V
    hn = hn_ref[...]
    qk = _dot(hn, w_ref[:, :2 * kw])
    v_all = _dot(hn, w_ref[:, 2 * kw:2 * kw + vw]).astype(BF16)
    g_all = _dot(hn, w_ref[:, 2 * kw + vw:])
    cos, sin = cos_ref[...], sin_ref[...]
    scale = MIX_QK ** -0.5
    for h in range(MIX_HEADS):
        vl = slice(h * MIX_V, (h + 1) * MIX_V)
        q = _rotary(qk[:, h * MIX_QK:(h + 1) * MIX_QK], cos, sin)
        k = _rotary(qk[:, kw + h * MIX_QK:kw + (h + 1) * MIX_QK], cos, sin) * scale
        v16 = v_all[:, vl]
        q16 = q.astype(BF16)
        inner = _dot_nt(q16, k.astype(BF16)) * dmat_ref[h]
        state = state_ref[h]
        out = _dot(inner.astype(BF16), v16) + _dot_nt((q * qdec_ref[h]).astype(BF16), state.astype(BF16))
        state_ref[h] = state * cdec_ref[h] + _dot_tn(v16, (k * kdec_ref[h]).astype(BF16))
        mu = jnp.mean(out, axis=-1, keepdims=True)
        cen = out - mu
        var = jnp.mean(cen * cen, axis=-1, keepdims=True)
        o = cen * lax.rsqrt(var + NORM_EPS) * ng_ref[:, vl]
        o_ref[:, vl] = (o * _silu(g_all[:, vl])).astype(o_ref.dtype)


def _rope_tables(seqlen):
    half = MIX_QK // 2
    inv_freq = ROPE_BASE ** (-jnp.arange(half, dtype=F32) / half)
    ang = jnp.arange(seqlen, dtype=F32)[:, None] * inv_freq[None, :]
    cos, sin = jnp.cos(ang), jnp.sin(ang)
    return jnp.concatenate([cos, cos], axis=1), jnp.concatenate([-sin, sin], axis=1)


def retention_mixer(hn, w_in, norm_g, batch, seqlen):
    c = MIX_TILE
    nt = seqlen // c
    d_model = hn.shape[1]
    kw, vw = MIX_HEADS * MIX_QK, MIX_HEADS * MIX_V
    assert seqlen % c == 0 and hn.shape[0] == batch * seqlen and w_in.shape == (d_model, 2 * kw + 2 * vw)
    cos, sin = _rope_tables(seqlen)
    log_gamma = jnp.log(1.0 - 2.0 ** (-5.0 - jnp.arange(MIX_HEADS, dtype=F32)))
    pos = jnp.arange(c, dtype=F32)
    diff = pos[:, None] - pos[None, :]
    dmat = jnp.where(diff >= 0, jnp.exp(log_gamma[:, None, None] * jnp.maximum(diff, 0.0)), 0.0)
    lanes = lambda col: jnp.broadcast_to(col[:, :, None], (MIX_HEADS, c, MIX_QK))
    qdec = lanes(jnp.exp(log_gamma[:, None] * (pos + 1.0)))
    kdec = lanes(jnp.exp(log_gamma[:, None] * (c - 1.0 - pos)))
    cdec = jnp.broadcast_to(jnp.exp(log_gamma * c)[:, None, None], (MIX_HEADS, 1, MIX_QK))
    consts = [dmat, qdec, kdec, cdec, norm_g.astype(F32).reshape(1, vw)]
    tok = lambda n: pl.BlockSpec((c, n), lambda b, i: (b * nt + i, 0))
    rope = pl.BlockSpec((c, MIX_QK), lambda b, i: (i, 0))
    return pl.pallas_call(
        _ret_kernel, grid=(batch, nt),
        in_specs=[tok(d_model), _resident(w_in.shape), rope, rope] + [_resident(x.shape) for x in consts],
        out_specs=tok(vw),
        out_shape=jax.ShapeDtypeStruct((batch * seqlen, vw), BF16),
        scratch_shapes=[pltpu.VMEM((MIX_HEADS, MIX_V, MIX_QK), F32)],
        compiler_params=_params("parallel", "arbitrary"), name="retention_mixer",
    )(hn, w_in, cos, sin, *consts)


GLA_CHUNK = 64
GLA_TAU = 16.0


def _log_sigmoid(x):
    return jnp.minimum(x, 0.0) - jnp.log1p(jnp.exp(-jnp.abs(x)))


def _gla_kernel(hn_ref, w_ref, gw_ref, gb_ref, ng_ref, cumsel_ref, lastsel_ref,
                o_ref, state_ref):
    tl = hn_ref.shape[0]
    nchunk = tl // GLA_CHUNK
    kw, vw = MIX_HEADS * MIX_QK, MIX_HEADS * MIX_V

    @pl.when(pl.program_id(1) == 0)
    def _():
        state_ref[...] = jnp.zeros(state_ref.shape, F32)

    hn = hn_ref[...]
    qk = _dot(hn, w_ref[:, :2 * kw])
    v_all = _dot(hn, w_ref[:, 2 * kw:2 * kw + vw])
    g_all = _dot(hn, w_ref[:, 2 * kw + vw:2 * kw + 2 * vw])
    a_lr = _dot(hn, w_ref[:, 2 * kw + 2 * vw:])
    log_a = _log_sigmoid(_dot(a_lr.astype(BF16), gw_ref[...]) + gb_ref[...]) / GLA_TAU
    bcum = _dot_select(cumsel_ref[...], log_a)
    blast = _dot_select(lastsel_ref[...], log_a)
    grow = jnp.exp(bcum)
    row = lax.broadcasted_iota(jnp.int32, (tl, tl), 0)
    col = lax.broadcasted_iota(jnp.int32, (tl, tl), 1)
    shift = GLA_CHUNK.bit_length() - 1
    keep = (row >= col) & (jnp.right_shift(row, shift) == jnp.right_shift(col, shift))
    col_chunk = jnp.right_shift(lax.broadcasted_iota(jnp.int32, (1, tl), 1), shift)
    scale = MIX_QK ** -0.5
    for h in range(MIX_HEADS):
        kl = slice(h * MIX_QK, (h + 1) * MIX_QK)
        vl = slice(h * MIX_V, (h + 1) * MIX_V)
        k = qk[:, kw + h * MIX_QK:kw + (h + 1) * MIX_QK]
        q_in = (qk[:, kl] * scale * grow[:, kl]).astype(BF16)
        k_in = (k * jnp.exp(-bcum[:, kl])).astype(BF16)
        k_out = (k * jnp.exp(blast[:, kl] - bcum[:, kl])).astype(BF16)
        v = v_all[:, vl]
        v16 = v.astype(BF16)
        att = jnp.where(keep, _dot_nt(q_in, k_in), 0.0)
        out = _dot(att.astype(BF16), v16)
        v_t = v.T
        stacked = jnp.concatenate(
            [jnp.where(col_chunk == c, v_t, 0.0) for c in range(nchunk)], axis=0).astype(BF16)
        kv = _dot(stacked, k_out)
        state = state_ref[h]
        inter = []
        for c in range(nchunk):
            rows = slice(c * GLA_CHUNK, (c + 1) * GLA_CHUNK)
            inter.append(_dot_nt(q_in[rows, :], state.astype(BF16)))
            decay = jnp.exp(blast[c * GLA_CHUNK:c * GLA_CHUNK + 1, kl])
            state = state * decay + kv[c * MIX_V:(c + 1) * MIX_V, :]
        state_ref[h] = state
        out = out + jnp.concatenate(inter, axis=0)
        o = out * lax.rsqrt(jnp.mean(out * out, axis=-1, keepdims=True) + NORM_EPS) * ng_ref[:, vl]
        o_ref[:, vl] = (o * _silu(g_all[:, vl])).astype(o_ref.dtype)


def gla_mixer(hn, w_in, gate_w, gate_b, norm_g, batch, seqlen):
    tl = MIX_TILE
    nt = seqlen // tl
    d_model = hn.shape[1]
    kw, vw = MIX_HEADS * MIX_QK, MIX_HEADS * MIX_V
    assert seqlen % tl == 0 and tl % GLA_CHUNK == 0 and hn.shape[0] == batch * seqlen
    assert w_in.shape == (d_model, 2 * kw + 2 * vw + LANES)
    rank = gate_w.shape[0]
    gw = jnp.pad(gate_w.astype(F32), ((0, LANES - rank), (0, 0))).astype(BF16)
    t = jnp.arange(tl)
    same = (t[:, None] // GLA_CHUNK) == (t[None, :] // GLA_CHUNK)
    cumsel = (same & (t[:, None] >= t[None, :])).astype(BF16)
    lastsel = same.astype(BF16)
    consts = [gw, gate_b.astype(F32).reshape(1, kw), norm_g.astype(F32).reshape(1, vw), cumsel, lastsel]
    tok = lambda n: pl.BlockSpec((tl, n), lambda b, i: (b * nt + i, 0))
    return pl.pallas_call(
        _gla_kernel, grid=(batch, nt),
        in_specs=[tok(d_model), _resident(w_in.shape)] + [_resident(x.shape) for x in consts],
        out_specs=tok(vw),
        out_shape=jax.ShapeDtypeStruct((batch * seqlen, vw), BF16),
        scratch_shapes=[pltpu.VMEM((MIX_HEADS, MIX_V, MIX_QK), F32)],
        compiler_params=_params("parallel", "arbitrary"), name="gla_mixer",
    )(hn, w_in, *consts)


WEIGHT_PREP_ROWS = 128


def _repack_kernel(w_ref, *o_refs, plans):
    for o_ref, pieces in zip(o_refs, plans):
        o_ref[...] = jnp.zeros(o_ref.shape, o_ref.dtype)
        for src_lo, src_hi, dst_lo in pieces:
            o_ref[:, dst_lo:dst_lo + src_hi - src_lo] = w_ref[:, src_lo:src_hi].astype(o_ref.dtype)


def repack_weight(w, plans, widths):
    rows, cols = w.shape
    rb = WEIGHT_PREP_ROWS
    assert rows % rb == 0 and all(n % LANES == 0 for n in widths)
    assert all(d % LANES == 0 and d + hi - lo <= n for p, n in zip(plans, widths) for lo, hi, d in p)
    return pl.pallas_call(
        functools.partial(_repack_kernel, plans=plans), grid=(rows // rb,),
        in_specs=[pl.BlockSpec((rb, cols), lambda i: (i, 0))],
        out_specs=[pl.BlockSpec((rb, n), lambda i: (i, 0)) for n in widths],
        out_shape=[jax.ShapeDtypeStruct((rows, n), BF16) for n in widths],
        compiler_params=_params("parallel"), name="repack_weight",
    )(w)


def kernel(x, norm_mix_g, norm_ffn_g, final_norm_g, ev_in_w, s5_lam_re, s5_lam_im, s5_log_step, s5_b_re, s5_b_im, s5_c_re, s5_c_im, s5_d, s5_glu_w, s5_glu_b, ssd_conv_w, ssd_conv_b, ssd_dt_bias, ssd_a_log, ssd_d, ssd_norm_g, ev_out_w, od_in_w, ret_norm_g, gla_gate_w, gla_gate_b, gla_norm_g, od_out_w, ffn_gate_w, ffn_up_w, ffn_down_w):
    bsz, seqlen, d = x.shape
    rows = bsz * seqlen
    depth = norm_mix_g.shape[0]
    h = x.reshape(rows, d)
    hn = None
    wg, wu, wd = ffn_gate_w.astype(BF16), ffn_up_w.astype(BF16), ffn_down_w.astype(BF16)
    for layer in range(depth):
        i = layer // 2
        if layer % 2 == 0:
            s5_w = s5_glu_w.shape[1]
            w = ev_in_w[i]
            total = w.shape[1]
            heads = ssd_dt_bias.shape[1]
            wide = total - heads - s5_w
            w_s5, w_ssd = repack_weight(
                w, (((0, s5_w, 0),), ((s5_w, total - heads, 0), (total - heads, total, wide))),
                (s5_w, wide + LANES))
            y_a = s5_mixer(h.reshape(bsz, seqlen, d), norm_mix_g[layer], w_s5, s5_lam_re[i], s5_lam_im[i],
                           s5_log_step[i], s5_b_re[i], s5_b_im[i], s5_c_re[i], s5_c_im[i], s5_d[i].reshape(-1),
                           s5_glu_w[i], s5_glu_b[i]).reshape(rows, s5_w)
            y_b = ssd_mixer(h, norm_mix_g[layer], w_ssd, ssd_conv_w[i], ssd_conv_b[i], ssd_dt_bias[i],
                            ssd_a_log[i], ssd_d[i], ssd_norm_g[i], bsz, seqlen)
            w_out = ev_out_w[i].astype(BF16)
        else:
            kw = gla_gate_w.shape[2]
            vw = ret_norm_g.shape[1]
            rank = gla_gate_w.shape[1]
            w = od_in_w[i]
            ret_w = 2 * kw + 2 * vw
            a_lo = ret_w + 2 * kw + vw
            qkv = a_lo - ret_w
            w_ret, w_gla = repack_weight(
                w, (((0, ret_w, 0),),
                    ((ret_w, a_lo, 0), (a_lo + rank, w.shape[1], qkv), (a_lo, a_lo + rank, qkv + vw))),
                (ret_w, qkv + vw + LANES))
            assert hn is not None, "odd layers follow an even layer"
            y_a = retention_mixer(hn, w_ret, ret_norm_g[i], bsz, seqlen)
            y_b = gla_mixer(hn, w_gla, gla_gate_w[i], gla_gate_b[i], gla_norm_g[i], bsz, seqlen)
            w_out = od_out_w[i].astype(BF16)
        if layer == depth - 1:
            h = mix_ffn(h, y_a, y_b, w_out, norm_ffn_g[layer], wg, wu, wd, layer, final_norm_g, final_norm=True)
        else:
            h, hn = mix_ffn(h, y_a, y_b, w_out, norm_ffn_g[layer], wg, wu, wd, layer,
                            norm_mix_g[layer + 1], final_norm=False)
    return h.reshape(bsz, seqlen, d)
```

```python
import functools

import jax
import jax.numpy as jnp
from jax import lax
from jax.experimental import pallas as pl
from jax.experimental.pallas import tpu as pltpu

F32 = jnp.float32
BF16 = jnp.bfloat16

NORM_EPS = 1e-6
VMEM_LIMIT_BYTES = 56 * 1024 * 1024
LANES = 128
SUBLANES = 8

ROW_TILE = 512
FFN_CHUNK = 256


def _rms(x, g):
    return x * lax.rsqrt(jnp.mean(x * x, axis=-1, keepdims=True) + NORM_EPS) * g


def _dot(a, b):
    return jnp.dot(a, b, preferred_element_type=F32)


def _resident(shape):
    zeros = (0,) * len(shape)
    return pl.BlockSpec(shape, lambda *_: zeros, pipeline_mode=pl.Buffered(1))


def _params(*semantics):
    return pltpu.CompilerParams(dimension_semantics=semantics,
                                vmem_limit_bytes=VMEM_LIMIT_BYTES)


def _mix_ffn_kernel(h_ref, ya_ref, yb_ref, wo_ref, g_ref, wg_ref, wu_ref, wd_ref,
                    ng_ref, *refs, final_norm):
    act_ref = refs[-1]
    na = ya_ref.shape[1]
    h1 = h_ref[...] + _dot(ya_ref[...], wo_ref[:na, :]) + _dot(yb_ref[...], wo_ref[na:, :])
    xn = _rms(h1, g_ref[...]).astype(BF16)
    hidden = wg_ref.shape[1]
    for j in range(hidden // FFN_CHUNK):
        cols = slice(j * FFN_CHUNK, (j + 1) * FFN_CHUNK)
        gate = _dot(xn, wg_ref[:, cols])
        up = _dot(xn, wu_ref[:, cols])
        act_ref[:, cols] = (gate * jax.nn.sigmoid(gate) * up).astype(BF16)
    out = h1 + _dot(act_ref[...], wd_ref[...])
    normed = _rms(out, ng_ref[...])
    if final_norm:
        refs[0][...] = normed
    else:
        refs[0][...] = out
        refs[1][...] = normed.astype(BF16)


def _resident_layer(shape, layer):
    tail = (0,) * (len(shape) - 1)
    return pl.BlockSpec((None,) + tuple(shape[1:]), lambda *_: (layer,) + tail, pipeline_mode=pl.Buffered(1))


def mix_ffn(h, ya, yb, wo, g, wg, wu, wd, layer, next_g, final_norm, row_tile=ROW_TILE):
    rows, d = h.shape
    hidden = wg.shape[2]
    assert rows % row_tile == 0 and hidden % FFN_CHUNK == 0
    assert wo.shape == (ya.shape[1] + yb.shape[1], d)
    row = lambda n: pl.BlockSpec((row_tile, n), lambda i: (i, 0))
    out_shape = [jax.ShapeDtypeStruct((rows, d), F32)]
    if not final_norm:
        out_shape.append(jax.ShapeDtypeStruct((rows, d), BF16))
    outs = pl.pallas_call(
        functools.partial(_mix_ffn_kernel, final_norm=final_norm),
        grid=(rows // row_tile,),
        in_specs=[row(d), row(ya.shape[1]), row(yb.shape[1]), _resident(wo.shape), _resident((1, d)),
                  _resident_layer(wg.shape, layer), _resident_layer(wu.shape, layer),
                  _resident_layer(wd.shape, layer), _resident((1, d))],
        out_specs=[row(d)] * len(out_shape),
        out_shape=out_shape,
        scratch_shapes=[pltpu.VMEM((row_tile, hidden), BF16)],
        compiler_params=_params("parallel"), name="mix_ffn",
    )(h, ya, yb, wo, g.reshape(1, d), wg, wu, wd, next_g.reshape(1, d))
    return outs[0] if final_norm else tuple(outs)


def _split3(a):
    hi = a.astype(BF16)
    r1 = a - hi.astype(F32)
    mid = r1.astype(BF16)
    lo = (r1 - mid.astype(F32)).astype(BF16)
    return hi, mid, lo


def _select_dot(a, sel):
    hi, mid, lo = _split3(a)
    return _dot(hi, sel) + _dot(mid, sel) + _dot(lo, sel)


def _dot_select(sel, a):
    hi, mid, lo = _split3(a)
    return _dot(sel, hi) + _dot(sel, mid) + _dot(sel, lo)


def _dot_nt(a, b):
    return lax.dot_general(a, b, (((1,), (1,)), ((), ())), preferred_element_type=F32)


def _dot_tn(a, b):
    return lax.dot_general(a, b, (((0,), (0,)), ((), ())), preferred_element_type=F32)


def _proj(hn, wt_ref, lo, hi):
    return _dot_nt(hn, wt_ref[lo:hi, :])


def _proj_narrow(hn, wt_ref, lo, hi):
    pad = jnp.zeros((LANES - (hi - lo), wt_ref.shape[1]), wt_ref.dtype)
    return _dot_nt(hn, jnp.concatenate([wt_ref[lo:hi, :], pad], axis=0))


def _silu(x):
    return x * jax.nn.sigmoid(x)


def _softplus(x):
    return jnp.maximum(x, 0.0) + jnp.log1p(jnp.exp(-jnp.abs(x)))


SSD_TILE = 256
SSD_HEADS = 16
SSD_HEADDIM = 64
SSD_GROUPS = 4
SSD_STATE = 64
SSD_CONV = 4


def _ssd_kernel(h_ref, hg_ref, wt_ref, cw_ref, cb_ref, dtb_ref, alog_ref, dsk_ref, ng_ref,
                expand_ref, tril_ref, hmask_ref, gmask_ref, bdmask_ref,
                o_ref, xpad_ref, state_ref, lhs_ref, rhs_ref, *, row0):
    cs = h_ref.shape[0]
    inner = o_ref.shape[1]
    gw = inner // SSD_GROUPS
    hpg = SSD_HEADS // SSD_GROUPS
    nslab = xpad_ref.shape[0]
    conv_dim = nslab * LANES

    hn = _rms(h_ref[...], hg_ref[...]).astype(BF16)
    z = _proj(hn, wt_ref, row0, row0 + inner)
    xbc = _proj(hn, wt_ref, row0 + inner, row0 + inner + conv_dim)
    dt_raw = _proj_narrow(hn, wt_ref, row0 + inner + conv_dim, row0 + inner + conv_dim + SSD_HEADS)

    @pl.when(pl.program_id(1) == 0)
    def _():
        xpad_ref[:, 0:8, :] = jnp.zeros((nslab, 8, LANES), F32)
        state_ref[...] = jnp.zeros(state_ref.shape, F32)

    convs = []
    for j in range(nslab):
        lanes = slice(j * LANES, (j + 1) * LANES)
        xpad_ref[j, 8:8 + cs, :] = xbc[:, lanes]
        conv = cb_ref[:, lanes]
        for k in range(SSD_CONV):
            lo = 8 - (SSD_CONV - 1) + k
            conv = conv + cw_ref[k:k + 1, lanes] * xpad_ref[j, lo:lo + cs, :]
        xpad_ref[j, 0:8, :] = xpad_ref[j, cs:cs + 8, :]
        convs.append(conv)
    xc = _silu(jnp.concatenate(convs, axis=1))
    xs = xc[:, :inner]
    bm = xc[:, inner:inner + SSD_GROUPS * SSD_STATE]
    cm = xc[:, inner + SSD_GROUPS * SSD_STATE:]

    dt = _softplus(dt_raw + dtb_ref[...])
    da = dt * -jnp.exp(alog_ref[...])
    a_cum = _dot_select(tril_ref[...], da)
    a_cum_t = a_cum.T
    expand = expand_ref[...]
    xdt = xs * _select_dot(dt, expand)
    a_x = _select_dot(a_cum, expand)
    a_last = a_x[cs - 1:cs, :]
    decay_in = jnp.exp(a_x)
    xd = (xdt * jnp.exp(a_last - a_x)).astype(BF16)
    xdt = xdt.astype(BF16)

    state = state_ref[...]
    y = _dot(cm.astype(BF16), state.astype(BF16)) * decay_in
    new_state = state * jnp.exp(a_last) + _dot(bm.T.astype(BF16), xd)
    state_ref[...] = new_state * bdmask_ref[...]

    row = lax.broadcasted_iota(jnp.int32, (cs, cs), 0)
    col = lax.broadcasted_iota(jnp.int32, (cs, cs), 1)
    causal = row >= col
    bm16 = bm.astype(BF16)
    ys = []
    for g in range(SSD_GROUPS):
        cb = _dot_nt((cm * gmask_ref[g:g + 1, :]).astype(BF16), bm16)
        xg = xdt[:, g * gw:(g + 1) * gw]
        for j in range(hpg):
            h = g * hpg + j
            diff = a_cum[:, h:h + 1] - a_cum_t[h:h + 1, :]
            seg = jnp.exp(jnp.where(causal, diff, -jnp.inf))
            lhs_ref[:, j * cs:(j + 1) * cs] = (cb * seg).astype(BF16)
            rhs_ref[j * cs:(j + 1) * cs, :] = xg * hmask_ref[j:j + 1, :]
        ys.append(_dot(lhs_ref[...], rhs_ref[...]))
    y = y + jnp.concatenate(ys, axis=1) + dsk_ref[...] * xs
    y = y * _silu(z)
    o_ref[...] = _rms(y, ng_ref[...]).astype(o_ref.dtype)


def ssd_mixer(h, h_norm_g, wt_in, row0, conv_w, conv_b, dt_bias, a_log, d_skip, norm_g, batch, seqlen):
    cs = SSD_TILE
    d_model = h.shape[1]
    inner = SSD_HEADS * SSD_HEADDIM
    conv_dim = conv_w.shape[1]
    gw = inner // SSD_GROUPS
    gs = SSD_GROUPS * SSD_STATE
    hpg = SSD_HEADS // SSD_GROUPS
    nt = seqlen // cs
    assert seqlen % cs == 0 and h.shape[0] == batch * seqlen and conv_dim % LANES == 0
    assert wt_in.shape[1] == d_model and row0 + inner + conv_dim + SSD_HEADS <= wt_in.shape[0]

    pad = lambda v: jnp.pad(v.astype(F32), (0, LANES - SSD_HEADS)).reshape(1, LANES)
    lane_head = jnp.arange(inner) // SSD_HEADDIM
    expand = (jnp.arange(LANES)[:, None] == lane_head[None, :]).astype(BF16)
    tril = (jnp.arange(cs)[:, None] >= jnp.arange(cs)[None, :]).astype(BF16)
    hmask = (jnp.arange(hpg)[:, None] == (jnp.arange(gw) // SSD_HEADDIM)[None, :]).astype(BF16)
    gmask = (jnp.arange(SSD_GROUPS)[:, None] == (jnp.arange(gs) // SSD_STATE)[None, :]).astype(F32)
    bdmask = ((jnp.arange(gs) // SSD_STATE)[:, None] == (jnp.arange(inner) // gw)[None, :]).astype(F32)

    tok = lambda n: pl.BlockSpec((cs, n), lambda b, c: (b * nt + c, 0))
    consts = [h_norm_g.astype(F32).reshape(1, d_model), wt_in,
              conv_w.astype(F32), conv_b.astype(F32).reshape(1, -1), pad(dt_bias), pad(a_log),
              jnp.repeat(d_skip.astype(F32), SSD_HEADDIM).reshape(1, inner), norm_g.astype(F32).reshape(1, inner),
              expand, tril, hmask, gmask, bdmask]
    return pl.pallas_call(
        functools.partial(_ssd_kernel, row0=row0), grid=(batch, nt),
        in_specs=[tok(d_model)] + [_resident(c.shape) for c in consts],
        out_specs=tok(inner),
        out_shape=jax.ShapeDtypeStruct((batch * seqlen, inner), BF16),
        scratch_shapes=[pltpu.VMEM((conv_dim // LANES, cs + 8, LANES), F32), pltpu.VMEM((gs, inner), F32),
                        pltpu.VMEM((cs, hpg * cs), BF16), pltpu.VMEM((hpg * cs, gw), BF16)],
        compiler_params=_params("parallel", "arbitrary"), name="ssd_mixer",
    )(h, *consts)


S5_TILE = 128
S5_CHUNK = 32
S5_GROUP = 16
S5_STATE = 64
S5_GROUPS_PER_BLOCK = 16


def _s5_kernel(h_ref, ng_ref, wt_ref, wb_ref, a_ref, wc_ref, d_ref, gw_ref, gb_ref, o_ref,
               bu_ref, x_ref, y_ref, abc_ref):
    bsz, tl, d_model = h_ref.shape
    width = d_ref.shape[1]
    nblk = wb_ref.shape[0]
    kw = wb_ref.shape[1]
    sw = a_ref.shape[2]
    rows = tl * bsz

    lanes = lambda kb: slice(kb * 2 * sw, (kb + 1) * 2 * sw)
    re = lambda kb: slice(kb * 2 * sw, kb * 2 * sw + sw)
    im = lambda kb: slice(kb * 2 * sw + sw, (kb + 1) * 2 * sw)

    @pl.when(pl.program_id(0) == 0)
    def _():
        x_ref[...] = jnp.zeros(x_ref.shape, F32)
        for kb in range(nblk):
            abc_ref[:, re(kb)] = jnp.broadcast_to(a_ref[kb, 0:1, :], (bsz, sw))
            abc_ref[:, im(kb)] = jnp.broadcast_to(a_ref[kb, 1:2, :], (bsz, sw))

    hn = _rms(h_ref[...].reshape(rows, d_model), ng_ref[...]).astype(BF16)
    u = _proj(hn, wt_ref, 0, width)
    u = jnp.swapaxes(u.reshape(bsz, tl, width), 0, 1).reshape(rows, width)
    u16 = u.astype(BF16)

    crows = S5_CHUNK * bsz
    nchunk = tl // S5_CHUNK

    def input_stage(j):
        rows_j = slice(j * crows, (j + 1) * crows)
        for kb in range(nblk):
            bu_ref[rows_j, lanes(kb)] = _dot(u16[rows_j, kb * kw:(kb + 1) * kw], wb_ref[kb])

    def scan_stage(j, state):
        for t in range(j * S5_CHUNK, (j + 1) * S5_CHUNK):
            r = slice(t * bsz, (t + 1) * bsz)
            new_state = []
            for kb in range(nblk):
                x_re, x_im = state[kb]
                a_re, a_im = abc_ref[:, re(kb)], abc_ref[:, im(kb)]
                n_re = a_re * x_re - a_im * x_im + bu_ref[r, re(kb)]
                n_im = a_re * x_im + a_im * x_re + bu_ref[r, im(kb)]
                bu_ref[r, re(kb)] = n_re
                bu_ref[r, im(kb)] = n_im
                new_state.append((n_re, n_im))
            state = new_state
        return state

    def output_stage(j):
        rows_j = slice(j * crows, (j + 1) * crows)
        ys = [_dot(bu_ref[rows_j, lanes(kb)].astype(BF16), wc_ref[kb]) for kb in range(nblk)]
        y = jnp.concatenate(ys, axis=1) + d_ref[...] * u[rows_j, :]
        z = jax.nn.gelu(y)
        y_ref[rows_j, :] = z * jax.nn.sigmoid(_dot(z.astype(BF16), gw_ref[...]) + gb_ref[...])

    state = [(x_ref[:, re(kb)], x_ref[:, im(kb)]) for kb in range(nblk)]
    input_stage(0)
    for j in range(nchunk):
        if j + 1 < nchunk:
            input_stage(j + 1)
        state = scan_stage(j, state)
        output_stage(j)
    for kb in range(nblk):
        x_ref[:, re(kb)] = state[kb][0]
        x_ref[:, im(kb)] = state[kb][1]
    o_ref[...] = jnp.swapaxes(y_ref[...].reshape(tl, bsz, width), 0, 1).astype(o_ref.dtype)


def _s5_weights(lam_re, lam_im, log_step, b_re, b_im, c_re, c_im):
    lr, li = lam_re.astype(F32), lam_im.astype(F32)
    step = jnp.exp(log_step.astype(F32))[:, None]
    mag = jnp.exp(lr * step)
    a_re, a_im = mag * jnp.cos(li * step), mag * jnp.sin(li * step)
    den = lr * lr + li * li
    k_re = ((a_re - 1.0) * lr + a_im * li) / den
    k_im = (a_im * lr - (a_re - 1.0) * li) / den
    br, bi = b_re.astype(F32), b_im.astype(F32)
    bb_re = k_re[..., None] * br - k_im[..., None] * bi
    bb_im = k_re[..., None] * bi + k_im[..., None] * br
    groups, nstate, nch = br.shape
    gpb = S5_GROUPS_PER_BLOCK
    nblk = groups // gpb
    eye = jnp.eye(gpb, dtype=F32)

    def in_block(bb):
        t = bb.reshape(nblk, gpb, nstate, nch).transpose(0, 1, 3, 2)
        t = t[:, :, :, None, :] * eye[None, :, None, :, None]
        return t.reshape(nblk, gpb * nch, gpb * nstate)

    def out_block(c):
        t = c.astype(F32).reshape(nblk, gpb, nch, nstate).transpose(0, 1, 3, 2)
        t = t[:, :, :, None, :] * eye[None, :, None, :, None]
        return t.reshape(nblk, gpb * nstate, gpb * nch)

    wb = jnp.concatenate([in_block(bb_re), in_block(bb_im)], axis=2).astype(BF16)
    wc = jnp.concatenate([out_block(c_re), -out_block(c_im)], axis=1).astype(BF16)
    a = jnp.stack([a_re.reshape(nblk, gpb * nstate), a_im.reshape(nblk, gpb * nstate)], axis=1)
    return wb, a, wc


def s5_mixer(h, norm_g, wt_in, lam_re, lam_im, log_step, b_re, b_im, c_re, c_im, d_skip, glu_w, glu_b):
    bsz, seqlen, d_model = h.shape
    width = glu_w.shape[0]
    tl = S5_TILE
    assert bsz == SUBLANES and seqlen % tl == 0
    wb, a, wc = _s5_weights(lam_re, lam_im, log_step, b_re, b_im, c_re, c_im)
    nstates = wb.shape[0] * wb.shape[2]
    consts = [norm_g.astype(F32).reshape(1, d_model), wt_in, wb, a, wc, d_skip.astype(F32).reshape(1, width),
              glu_w.astype(BF16), glu_b.astype(F32).reshape(1, width)]
    tok = lambda n: pl.BlockSpec((bsz, tl, n), lambda i: (0, i, 0))
    return pl.pallas_call(
        _s5_kernel, grid=(seqlen // tl,),
        in_specs=[tok(d_model)] + [_resident(c.shape) for c in consts],
        out_specs=tok(width),
        out_shape=jax.ShapeDtypeStruct((bsz, seqlen, width), BF16),
        scratch_shapes=[pltpu.VMEM((tl * bsz, nstates), F32), pltpu.VMEM((bsz, nstates), F32),
                        pltpu.VMEM((tl * bsz, width), F32), pltpu.VMEM((bsz, nstates), F32)],
        compiler_params=_params("arbitrary"), name="s5_mixer",
    )(h, *consts)


MIX_TILE = 256
MIX_HEADS = 4
MIX_QK = 128
MIX_V = 256
ROPE_BASE = 10000.0


def _rotary(t, cos, sin_signed):
    return t * cos + pltpu.roll(t, MIX_QK // 2, axis=1) * sin_signed


def _ret_kernel(hn_ref, wt_ref, cos_ref, sin_ref, dmat_ref, qdec_ref, kdec_ref, cdec_ref,
                ng_ref, o_ref, state_ref, *, row0):
    @pl.when(pl.program_id(1) == 0)
    def _():
        state_ref[...] = jnp.zeros(state_ref.shape, F32)

    kw, vw = MIX_HEADS * MIX_QK, MIX_HEADS * MIX_V
    hn = hn_ref[...]
    qk = _proj(hn, wt_ref, row0, row0 + 2 * kw)
    v_all = _proj(hn, wt_ref, row0 + 2 * kw, row0 + 2 * kw + vw).astype(BF16)
    g_all = _proj(hn, wt_ref, row0 + 2 * kw + vw, row0 + 2 * kw + 2 * vw)
    cos, sin = cos_ref[...], sin_ref[...]
    scale = MIX_QK ** -0.5
    for h in range(MIX_HEADS):
        vl = slice(h * MIX_V, (h + 1) * MIX_V)
        q = _rotary(qk[:, h * MIX_QK:(h + 1) * MIX_QK], cos, sin)
        k = _rotary(qk[:, kw + h * MIX_QK:kw + (h + 1) * MIX_QK], cos, sin) * scale
        v16 = v_all[:, vl]
        q16 = q.astype(BF16)
        inner = _dot_nt(q16, k.astype(BF16)) * dmat_ref[h]
        state = state_ref[h]
        out = _dot(inner.astype(BF16), v16) + _dot_nt((q * qdec_ref[h]).astype(BF16), state.astype(BF16))
        state_ref[h] = state * cdec_ref[h] + _dot_tn(v16, (k * kdec_ref[h]).astype(BF16))
        mu = jnp.mean(out, axis=-1, keepdims=True)
        cen = out - mu
        var = jnp.mean(cen * cen, axis=-1, keepdims=True)
        o = cen * lax.rsqrt(var + NORM_EPS) * ng_ref[:, vl]
        o_ref[:, vl] = (o * _silu(g_all[:, vl])).astype(o_ref.dtype)


def _rope_tables(seqlen):
    half = MIX_QK // 2
    inv_freq = ROPE_BASE ** (-jnp.arange(half, dtype=F32) / half)
    ang = jnp.arange(seqlen, dtype=F32)[:, None] * inv_freq[None, :]
    cos, sin = jnp.cos(ang), jnp.sin(ang)
    return jnp.concatenate([cos, cos], axis=1), jnp.concatenate([-sin, sin], axis=1)


def retention_mixer(hn, wt_in, row0, norm_g, batch, seqlen):
    c = MIX_TILE
    nt = seqlen // c
    d_model = hn.shape[1]
    kw, vw = MIX_HEADS * MIX_QK, MIX_HEADS * MIX_V
    assert seqlen % c == 0 and hn.shape[0] == batch * seqlen
    assert wt_in.shape[1] == d_model and row0 + 2 * kw + 2 * vw <= wt_in.shape[0]
    cos, sin = _rope_tables(seqlen)
    log_gamma = jnp.log(1.0 - 2.0 ** (-5.0 - jnp.arange(MIX_HEADS, dtype=F32)))
    pos = jnp.arange(c, dtype=F32)
    diff = pos[:, None] - pos[None, :]
    dmat = jnp.where(diff >= 0, jnp.exp(log_gamma[:, None, None] * jnp.maximum(diff, 0.0)), 0.0)
    lanes = lambda col: jnp.broadcast_to(col[:, :, None], (MIX_HEADS, c, MIX_QK))
    qdec = lanes(jnp.exp(log_gamma[:, None] * (pos + 1.0)))
    kdec = lanes(jnp.exp(log_gamma[:, None] * (c - 1.0 - pos)))
    cdec = jnp.broadcast_to(jnp.exp(log_gamma * c)[:, None, None], (MIX_HEADS, 1, MIX_QK))
    consts = [dmat, qdec, kdec, cdec, norm_g.astype(F32).reshape(1, vw)]
    tok = lambda n: pl.BlockSpec((c, n), lambda b, i: (b * nt + i, 0))
    rope = pl.BlockSpec((c, MIX_QK), lambda b, i: (i, 0))
    return pl.pallas_call(
        functools.partial(_ret_kernel, row0=row0), grid=(batch, nt),
        in_specs=[tok(d_model), _resident(wt_in.shape), rope, rope] + [_resident(x.shape) for x in consts],
        out_specs=tok(vw),
        out_shape=jax.ShapeDtypeStruct((batch * seqlen, vw), BF16),
        scratch_shapes=[pltpu.VMEM((MIX_HEADS, MIX_V, MIX_QK), F32)],
        compiler_params=_params("parallel", "arbitrary"), name="retention_mixer",
    )(hn, wt_in, cos, sin, *consts)


GLA_CHUNK = 64
GLA_TAU = 16.0


def _log_sigmoid(x):
    return jnp.minimum(x, 0.0) - jnp.log1p(jnp.exp(-jnp.abs(x)))


def _gla_kernel(hn_ref, wt_ref, gw_ref, gb_ref, ng_ref, cumsel_ref, lastsel_ref,
                o_ref, state_ref, *, row0, rank):
    tl = hn_ref.shape[0]
    nchunk = tl // GLA_CHUNK
    kw, vw = MIX_HEADS * MIX_QK, MIX_HEADS * MIX_V

    @pl.when(pl.program_id(1) == 0)
    def _():
        state_ref[...] = jnp.zeros(state_ref.shape, F32)

    hn = hn_ref[...]
    a_lo = row0 + 2 * kw + vw
    qk = _proj(hn, wt_ref, row0, row0 + 2 * kw)
    v_all = _proj(hn, wt_ref, row0 + 2 * kw, a_lo)
    a_lr = _proj_narrow(hn, wt_ref, a_lo, a_lo + rank)
    g_all = _proj(hn, wt_ref, a_lo + rank, a_lo + rank + vw)
    log_a = _log_sigmoid(_dot(a_lr.astype(BF16), gw_ref[...]) + gb_ref[...]) / GLA_TAU
    bcum = _dot_select(cumsel_ref[...], log_a)
    blast = _dot_select(lastsel_ref[...], log_a)
    grow = jnp.exp(bcum)
    row = lax.broadcasted_iota(jnp.int32, (tl, tl), 0)
    col = lax.broadcasted_iota(jnp.int32, (tl, tl), 1)
    shift = GLA_CHUNK.bit_length() - 1
    keep = (row >= col) & (jnp.right_shift(row, shift) == jnp.right_shift(col, shift))
    col_chunk = jnp.right_shift(lax.broadcasted_iota(jnp.int32, (1, tl), 1), shift)
    scale = MIX_QK ** -0.5
    for h in range(MIX_HEADS):
        kl = slice(h * MIX_QK, (h + 1) * MIX_QK)
        vl = slice(h * MIX_V, (h + 1) * MIX_V)
        k = qk[:, kw + h * MIX_QK:kw + (h + 1) * MIX_QK]
        q_in = (qk[:, kl] * scale * grow[:, kl]).astype(BF16)
        k_in = (k * jnp.exp(-bcum[:, kl])).astype(BF16)
        k_out = (k * jnp.exp(blast[:, kl] - bcum[:, kl])).astype(BF16)
        v = v_all[:, vl]
        v16 = v.astype(BF16)
        att = jnp.where(keep, _dot_nt(q_in, k_in), 0.0)
        out = _dot(att.astype(BF16), v16)
        v_t = v.T
        stacked = jnp.concatenate(
            [jnp.where(col_chunk == c, v_t, 0.0) for c in range(nchunk)], axis=0).astype(BF16)
        kv = _dot(stacked, k_out)
        state = state_ref[h]
        inter = []
        for c in range(nchunk):
            rows = slice(c * GLA_CHUNK, (c + 1) * GLA_CHUNK)
            inter.append(_dot_nt(q_in[rows, :], state.astype(BF16)))
            decay = jnp.exp(blast[c * GLA_CHUNK:c * GLA_CHUNK + 1, kl])
            state = state * decay + kv[c * MIX_V:(c + 1) * MIX_V, :]
        state_ref[h] = state
        out = out + jnp.concatenate(inter, axis=0)
        o = out * lax.rsqrt(jnp.mean(out * out, axis=-1, keepdims=True) + NORM_EPS) * ng_ref[:, vl]
        o_ref[:, vl] = (o * _silu(g_all[:, vl])).astype(o_ref.dtype)


def gla_mixer(hn, wt_in, row0, gate_w, gate_b, norm_g, batch, seqlen):
    tl = MIX_TILE
    nt = seqlen // tl
    d_model = hn.shape[1]
    kw, vw = MIX_HEADS * MIX_QK, MIX_HEADS * MIX_V
    assert seqlen % tl == 0 and tl % GLA_CHUNK == 0 and hn.shape[0] == batch * seqlen
    rank = gate_w.shape[0]
    assert wt_in.shape[1] == d_model and row0 + 2 * kw + 2 * vw + rank <= wt_in.shape[0]
    gw = jnp.pad(gate_w.astype(F32), ((0, LANES - rank), (0, 0))).astype(BF16)
    t = jnp.arange(tl)
    same = (t[:, None] // GLA_CHUNK) == (t[None, :] // GLA_CHUNK)
    cumsel = (same & (t[:, None] >= t[None, :])).astype(BF16)
    lastsel = same.astype(BF16)
    consts = [gw, gate_b.astype(F32).reshape(1, kw), norm_g.astype(F32).reshape(1, vw), cumsel, lastsel]
    tok = lambda n: pl.BlockSpec((tl, n), lambda b, i: (b * nt + i, 0))
    return pl.pallas_call(
        functools.partial(_gla_kernel, row0=row0, rank=rank), grid=(batch, nt),
        in_specs=[tok(d_model), _resident(wt_in.shape)] + [_resident(x.shape) for x in consts],
        out_specs=tok(vw),
        out_shape=jax.ShapeDtypeStruct((batch * seqlen, vw), BF16),
        scratch_shapes=[pltpu.VMEM((MIX_HEADS, MIX_V, MIX_QK), F32)],
        compiler_params=_params("parallel", "arbitrary"), name="gla_mixer",
    )(hn, wt_in, *consts)


def kernel(x, norm_mix_g, norm_ffn_g, final_norm_g, ev_in_w, s5_lam_re, s5_lam_im, s5_log_step, s5_b_re, s5_b_im, s5_c_re, s5_c_im, s5_d, s5_glu_w, s5_glu_b, ssd_conv_w, ssd_conv_b, ssd_dt_bias, ssd_a_log, ssd_d, ssd_norm_g, ev_out_w, od_in_w, ret_norm_g, gla_gate_w, gla_gate_b, gla_norm_g, od_out_w, ffn_gate_w, ffn_up_w, ffn_down_w):
    bsz, seqlen, d = x.shape
    rows = bsz * seqlen
    depth = norm_mix_g.shape[0]
    h = x.reshape(rows, d)
    hn = None
    wg, wu, wd = ffn_gate_w.astype(BF16), ffn_up_w.astype(BF16), ffn_down_w.astype(BF16)
    for layer in range(depth):
        i = layer // 2
        if layer % 2 == 0:
            s5_w = s5_glu_w.shape[1]
            wt_in = jnp.swapaxes(ev_in_w[i], 0, 1).astype(BF16)
            y_a = s5_mixer(h.reshape(bsz, seqlen, d), norm_mix_g[layer], wt_in, s5_lam_re[i], s5_lam_im[i],
                           s5_log_step[i], s5_b_re[i], s5_b_im[i], s5_c_re[i], s5_c_im[i], s5_d[i].reshape(-1),
                           s5_glu_w[i], s5_glu_b[i]).reshape(rows, s5_w)
            y_b = ssd_mixer(h, norm_mix_g[layer], wt_in, s5_w, ssd_conv_w[i], ssd_conv_b[i], ssd_dt_bias[i],
                            ssd_a_log[i], ssd_d[i], ssd_norm_g[i], bsz, seqlen)
            w_out = ev_out_w[i].astype(BF16)
        else:
            kw = gla_gate_w.shape[2]
            vw = ret_norm_g.shape[1]
            wt_in = jnp.swapaxes(od_in_w[i], 0, 1).astype(BF16)
            assert hn is not None, "odd layers follow an even layer"
            y_a = retention_mixer(hn, wt_in, 0, ret_norm_g[i], bsz, seqlen)
            y_b = gla_mixer(hn, wt_in, 2 * kw + 2 * vw, gla_gate_w[i], gla_gate_b[i], gla_norm_g[i], bsz, seqlen)
            w_out = od_out_w[i].astype(BF16)
        if layer == depth - 1:
            h = mix_ffn(h, y_a, y_b, w_out, norm_ffn_g[layer], wg, wu, wd, layer, final_norm_g, final_norm=True)
        else:
            h, hn = mix_ffn(h, y_a, y_b, w_out, norm_ffn_g[layer], wg, wu, wd, layer,
                            norm_mix_g[layer + 1], final_norm=False)
    return h.reshape(bsz, seqlen, d)
```

```python
import functools

import jax
import jax.numpy as jnp
import numpy as np
from jax import lax
from jax.experimental import pallas as pl
from jax.experimental.pallas import tpu as pltpu

F32 = jnp.float32
BF16 = jnp.bfloat16

NORM_EPS = 1e-6
VMEM_LIMIT_BYTES = 56 * 1024 * 1024
LANES = 128
SUBLANES = 8

ROW_TILE = 512
FFN_CHUNK = 256


def _rms(x, g):
    return x * lax.rsqrt(jnp.mean(x * x, axis=-1, keepdims=True) + NORM_EPS) * g


def _dot(a, b):
    return jnp.dot(a, b, preferred_element_type=F32)


def _resident(shape):
    zeros = (0,) * len(shape)
    return pl.BlockSpec(shape, lambda *_: zeros, pipeline_mode=pl.Buffered(1))


def _params(*semantics):
    return pltpu.CompilerParams(dimension_semantics=semantics,
                                vmem_limit_bytes=VMEM_LIMIT_BYTES)


def _mix_ffn_kernel(h_ref, ya_ref, yb_ref, wo_ref, g_ref, wg_ref, wu_ref, wd_ref,
                    ng_ref, *refs, final_norm):
    act_ref = refs[-1]
    na = ya_ref.shape[1]
    h1 = h_ref[...] + _dot(ya_ref[...], wo_ref[:na, :]) + _dot(yb_ref[...], wo_ref[na:, :])
    xn = _rms(h1, g_ref[...]).astype(BF16)
    hidden = wg_ref.shape[1]
    for j in range(hidden // FFN_CHUNK):
        cols = slice(j * FFN_CHUNK, (j + 1) * FFN_CHUNK)
        gate = _dot(xn, wg_ref[:, cols])
        up = _dot(xn, wu_ref[:, cols])
        act_ref[:, cols] = (gate * jax.nn.sigmoid(gate) * up).astype(BF16)
    out = h1 + _dot(act_ref[...], wd_ref[...])
    normed = _rms(out, ng_ref[...])
    if final_norm:
        refs[0][...] = normed
    else:
        refs[0][...] = out
        refs[1][...] = normed.astype(BF16)


def _resident_layer(shape, layer):
    tail = (0,) * (len(shape) - 1)
    return pl.BlockSpec((None,) + tuple(shape[1:]), lambda *_: (layer,) + tail, pipeline_mode=pl.Buffered(1))


def mix_ffn(h, ya, yb, wo, g, wg, wu, wd, layer, next_g, final_norm, row_tile=ROW_TILE):
    rows, d = h.shape
    hidden = wg.shape[2]
    assert rows % row_tile == 0 and hidden % FFN_CHUNK == 0
    assert wo.shape == (ya.shape[1] + yb.shape[1], d)
    row = lambda n: pl.BlockSpec((row_tile, n), lambda i: (i, 0))
    out_shape = [jax.ShapeDtypeStruct((rows, d), F32)]
    if not final_norm:
        out_shape.append(jax.ShapeDtypeStruct((rows, d), BF16))
    outs = pl.pallas_call(
        functools.partial(_mix_ffn_kernel, final_norm=final_norm),
        grid=(rows // row_tile,),
        in_specs=[row(d), row(ya.shape[1]), row(yb.shape[1]), _resident(wo.shape), _resident((1, d)),
                  _resident_layer(wg.shape, layer), _resident_layer(wu.shape, layer),
                  _resident_layer(wd.shape, layer), _resident((1, d))],
        out_specs=[row(d)] * len(out_shape),
        out_shape=out_shape,
        scratch_shapes=[pltpu.VMEM((row_tile, hidden), BF16)],
        compiler_params=_params("parallel"), name="mix_ffn",
    )(h, ya, yb, wo, g.reshape(1, d), wg, wu, wd, next_g.reshape(1, d))
    return outs[0] if final_norm else tuple(outs)


def _split3(a):
    hi = a.astype(BF16)
    r1 = a - hi.astype(F32)
    mid = r1.astype(BF16)
    lo = (r1 - mid.astype(F32)).astype(BF16)
    return hi, mid, lo


def _select_dot(a, sel):
    hi, mid, lo = _split3(a)
    return _dot(hi, sel) + _dot(mid, sel) + _dot(lo, sel)


def _dot_select(sel, a):
    hi, mid, lo = _split3(a)
    return _dot(sel, hi) + _dot(sel, mid) + _dot(sel, lo)


def _dot_nt(a, b):
    return lax.dot_general(a, b, (((1,), (1,)), ((), ())), preferred_element_type=F32)


def _dot_tn(a, b):
    return lax.dot_general(a, b, (((0,), (0,)), ((), ())), preferred_element_type=F32)


def _proj(hn, wt_ref, lo, hi):
    return _dot_nt(hn, wt_ref[lo:hi, :])


def _proj_narrow(hn, wt_ref, lo, hi):
    pad = jnp.zeros((LANES - (hi - lo), wt_ref.shape[1]), wt_ref.dtype)
    return _dot_nt(hn, jnp.concatenate([wt_ref[lo:hi, :], pad], axis=0))


def _silu(x):
    return x * jax.nn.sigmoid(x)


def _softplus(x):
    return jnp.maximum(x, 0.0) + jnp.log1p(jnp.exp(-jnp.abs(x)))


SSD_TILE = 256
SSD_HEADS = 16
SSD_HEADDIM = 64
SSD_GROUPS = 4
SSD_STATE = 64
SSD_CONV = 4
PROJ_PIECE = 256


def _proj_pieces(lo, hi, narrow_rows):
    pieces = [(a, min(a + PROJ_PIECE, hi), False) for a in range(lo, hi, PROJ_PIECE)]
    return pieces + [(hi, hi + narrow_rows, True)]


def _ssd_kernel(h0_ref, hnext_ref, hg_ref, wt_ref, cw_ref, cb_ref, dtb_ref, alog_ref, dsk_ref, ng_ref,
                expand_ref, tril_ref, hmask_ref, gmask_ref, bdmask_ref,
                o_ref, xpad_ref, state_ref, lhs_ref, rhs_ref, p_ref, z_ref, *, row0, nt):
    cs = hnext_ref.shape[0]
    inner = o_ref.shape[1]
    gw = inner // SSD_GROUPS
    hpg = SSD_HEADS // SSD_GROUPS
    nslab = xpad_ref.shape[0]
    conv_dim = nslab * LANES
    step = pl.program_id(0)

    def project(h_tile, pieces):
        hn = _rms(h_tile, hg_ref[...]).astype(BF16)

        def emit(count=1):
            for _ in range(min(count, len(pieces))):
                lo, hi, narrow = pieces.pop(0)
                col = lo - row0
                if narrow:
                    p_ref[:, col:col + LANES] = _proj_narrow(hn, wt_ref, lo, hi)
                else:
                    p_ref[:, col:col + hi - lo] = _proj(hn, wt_ref, lo, hi)
        return emit

    all_pieces = lambda: _proj_pieces(row0, row0 + inner + conv_dim, SSD_HEADS)

    @pl.when(step == 0)
    def _():
        project(h0_ref[...], all_pieces())(len(all_pieces()))

    @pl.when(step % nt == 0)
    def _():
        xpad_ref[:, 0:8, :] = jnp.zeros((nslab, 8, LANES), F32)
        state_ref[...] = jnp.zeros(state_ref.shape, F32)

    dt_raw = p_ref[:, inner + conv_dim:]
    z_ref[...] = p_ref[:, :inner]
    for j in range(nslab):
        xpad_ref[j, 8:8 + cs, :] = p_ref[:, inner + j * LANES:inner + (j + 1) * LANES]
    emit_next = project(hnext_ref[...], all_pieces())

    convs = []
    for j in range(nslab):
        lanes = slice(j * LANES, (j + 1) * LANES)
        conv = cb_ref[:, lanes]
        for k in range(SSD_CONV):
            lo = 8 - (SSD_CONV - 1) + k
            conv = conv + cw_ref[k:k + 1, lanes] * xpad_ref[j, lo:lo + cs, :]
        xpad_ref[j, 0:8, :] = xpad_ref[j, cs:cs + 8, :]
        convs.append(conv)
        if j % 3 == 2:
            emit_next()
    xc = _silu(jnp.concatenate(convs, axis=1))
    xs = xc[:, :inner]
    bm = xc[:, inner:inner + SSD_GROUPS * SSD_STATE]
    cm = xc[:, inner + SSD_GROUPS * SSD_STATE:]

    dt = _softplus(dt_raw + dtb_ref[...])
    da = dt * -jnp.exp(alog_ref[...])
    a_cum = _dot_select(tril_ref[...], da)
    a_cum_t = a_cum.T
    expand = expand_ref[...]
    xdt = xs * _select_dot(dt, expand)
    a_x = _select_dot(a_cum, expand)
    a_last = a_x[cs - 1:cs, :]
    decay_in = jnp.exp(a_x)
    xd = (xdt * jnp.exp(a_last - a_x)).astype(BF16)
    xdt = xdt.astype(BF16)

    state = state_ref[...]
    y = _dot(cm.astype(BF16), state.astype(BF16)) * decay_in
    new_state = state * jnp.exp(a_last) + _dot(bm.T.astype(BF16), xd)
    state_ref[...] = new_state * bdmask_ref[...]

    row = lax.broadcasted_iota(jnp.int32, (cs, cs), 0)
    col = lax.broadcasted_iota(jnp.int32, (cs, cs), 1)
    causal = row >= col
    bm16 = bm.astype(BF16)
    ys = []
    for g in range(SSD_GROUPS):
        cb = _dot_nt((cm * gmask_ref[g:g + 1, :]).astype(BF16), bm16)
        xg = xdt[:, g * gw:(g + 1) * gw]
        for j in range(hpg):
            h = g * hpg + j
            diff = a_cum[:, h:h + 1] - a_cum_t[h:h + 1, :]
            seg = jnp.exp(jnp.where(causal, diff, -jnp.inf))
            lhs_ref[:, j * cs:(j + 1) * cs] = (cb * seg).astype(BF16)
            rhs_ref[j * cs:(j + 1) * cs, :] = xg * hmask_ref[j:j + 1, :]
        ys.append(_dot(lhs_ref[...], rhs_ref[...]))
        emit_next(2)
    y = y + jnp.concatenate(ys, axis=1) + dsk_ref[...] * xs
    y = y * _silu(z_ref[...])
    o_ref[...] = _rms(y, ng_ref[...]).astype(o_ref.dtype)
    emit_next(len(all_pieces()))


def ssd_mixer(h, h_norm_g, wt_in, row0, conv_w, conv_b, dt_bias, a_log, d_skip, norm_g, batch, seqlen):
    cs = SSD_TILE
    d_model = h.shape[1]
    inner = SSD_HEADS * SSD_HEADDIM
    conv_dim = conv_w.shape[1]
    gw = inner // SSD_GROUPS
    gs = SSD_GROUPS * SSD_STATE
    hpg = SSD_HEADS // SSD_GROUPS
    nt = seqlen // cs
    assert seqlen % cs == 0 and h.shape[0] == batch * seqlen and conv_dim % LANES == 0
    assert wt_in.shape[1] == d_model and row0 + inner + conv_dim + SSD_HEADS <= wt_in.shape[0]

    pad = lambda v: jnp.pad(v.astype(F32), (0, LANES - SSD_HEADS)).reshape(1, LANES)
    lane_head = np.arange(inner) // SSD_HEADDIM
    expand = jnp.asarray(np.arange(LANES)[:, None] == lane_head[None, :], BF16)
    tril = jnp.asarray(np.arange(cs)[:, None] >= np.arange(cs)[None, :], BF16)
    hmask = jnp.asarray(np.arange(hpg)[:, None] == (np.arange(gw) // SSD_HEADDIM)[None, :], BF16)
    gmask = jnp.asarray(np.arange(SSD_GROUPS)[:, None] == (np.arange(gs) // SSD_STATE)[None, :], F32)
    bdmask = jnp.asarray((np.arange(gs) // SSD_STATE)[:, None] == (np.arange(inner) // gw)[None, :], F32)

    ntiles = batch * nt
    consts = [h_norm_g.astype(F32).reshape(1, d_model), wt_in,
              conv_w.astype(F32), conv_b.astype(F32).reshape(1, -1), pad(dt_bias), pad(a_log),
              jnp.repeat(d_skip.astype(F32), SSD_HEADDIM).reshape(1, inner), norm_g.astype(F32).reshape(1, inner),
              expand, tril, hmask, gmask, bdmask]
    return pl.pallas_call(
        functools.partial(_ssd_kernel, row0=row0, nt=nt), grid=(ntiles,),
        in_specs=[pl.BlockSpec((cs, d_model), lambda s: (0, 0)),
                  pl.BlockSpec((cs, d_model), lambda s: (jnp.minimum(s + 1, ntiles - 1), 0))]
        + [_resident(c.shape) for c in consts],
        out_specs=pl.BlockSpec((cs, inner), lambda s: (s, 0)),
        out_shape=jax.ShapeDtypeStruct((batch * seqlen, inner), BF16),
        scratch_shapes=[pltpu.VMEM((conv_dim // LANES, cs + 8, LANES), F32), pltpu.VMEM((gs, inner), F32),
                        pltpu.VMEM((cs, hpg * cs), BF16), pltpu.VMEM((hpg * cs, gw), BF16),
                        pltpu.VMEM((cs, inner + conv_dim + LANES), F32), pltpu.VMEM((cs, inner), F32)],
        compiler_params=_params("arbitrary"), name="ssd_mixer",
    )(h, h, *consts)


S5_TILE = 128
S5_CHUNK = 32
S5_GROUP = 16
S5_STATE = 64
S5_GROUPS_PER_BLOCK = 16


def _s5_kernel(h_ref, ng_ref, wt_ref, wb_ref, a_ref, wc_ref, d_ref, gw_ref, gb_ref, o_ref,
               bu_ref, x_ref, y_ref, abc_ref):
    bsz, tl, d_model = h_ref.shape
    width = d_ref.shape[1]
    nblk = wb_ref.shape[0]
    kw = wb_ref.shape[1]
    sw = a_ref.shape[2]
    rows = tl * bsz

    lanes = lambda kb: slice(kb * 2 * sw, (kb + 1) * 2 * sw)
    re = lambda kb: slice(kb * 2 * sw, kb * 2 * sw + sw)
    im = lambda kb: slice(kb * 2 * sw + sw, (kb + 1) * 2 * sw)

    @pl.when(pl.program_id(0) == 0)
    def _():
        x_ref[...] = jnp.zeros(x_ref.shape, F32)
        for kb in range(nblk):
            abc_ref[:, re(kb)] = jnp.broadcast_to(a_ref[kb, 0:1, :], (bsz, sw))
            abc_ref[:, im(kb)] = jnp.broadcast_to(a_ref[kb, 1:2, :], (bsz, sw))

    hn = _rms(h_ref[...].reshape(rows, d_model), ng_ref[...]).astype(BF16)
    u = _proj(hn, wt_ref, 0, width)
    u = jnp.swapaxes(u.reshape(bsz, tl, width), 0, 1).reshape(rows, width)
    u16 = u.astype(BF16)

    crows = S5_CHUNK * bsz
    nchunk = tl // S5_CHUNK

    def input_stage(j):
        rows_j = slice(j * crows, (j + 1) * crows)
        for kb in range(nblk):
            bu_ref[rows_j, lanes(kb)] = _dot(u16[rows_j, kb * kw:(kb + 1) * kw], wb_ref[kb])

    def scan_stage(j, state):
        for t in range(j * S5_CHUNK, (j + 1) * S5_CHUNK):
            r = slice(t * bsz, (t + 1) * bsz)
            new_state = []
            for kb in range(nblk):
                x_re, x_im = state[kb]
                a_re, a_im = abc_ref[:, re(kb)], abc_ref[:, im(kb)]
                n_re = a_re * x_re - a_im * x_im + bu_ref[r, re(kb)]
                n_im = a_re * x_im + a_im * x_re + bu_ref[r, im(kb)]
                bu_ref[r, re(kb)] = n_re
                bu_ref[r, im(kb)] = n_im
                new_state.append((n_re, n_im))
            state = new_state
        return state

    def output_stage(j):
        rows_j = slice(j * crows, (j + 1) * crows)
        ys = [_dot(bu_ref[rows_j, lanes(kb)].astype(BF16), wc_ref[kb]) for kb in range(nblk)]
        y = jnp.concatenate(ys, axis=1) + d_ref[...] * u[rows_j, :]
        z = jax.nn.gelu(y)
        y_ref[rows_j, :] = z * jax.nn.sigmoid(_dot(z.astype(BF16), gw_ref[...]) + gb_ref[...])

    state = [(x_ref[:, re(kb)], x_ref[:, im(kb)]) for kb in range(nblk)]
    input_stage(0)
    for j in range(nchunk):
        if j + 1 < nchunk:
            input_stage(j + 1)
        state = scan_stage(j, state)
        output_stage(j)
    for kb in range(nblk):
        x_ref[:, re(kb)] = state[kb][0]
        x_ref[:, im(kb)] = state[kb][1]
    o_ref[...] = jnp.swapaxes(y_ref[...].reshape(tl, bsz, width), 0, 1).astype(o_ref.dtype)


def _s5_weights(lam_re, lam_im, log_step, b_re, b_im, c_re, c_im):
    lr, li = lam_re.astype(F32), lam_im.astype(F32)
    step = jnp.exp(log_step.astype(F32))[:, None]
    mag = jnp.exp(lr * step)
    a_re, a_im = mag * jnp.cos(li * step), mag * jnp.sin(li * step)
    den = lr * lr + li * li
    k_re = ((a_re - 1.0) * lr + a_im * li) / den
    k_im = (a_im * lr - (a_re - 1.0) * li) / den
    br, bi = b_re.astype(F32), b_im.astype(F32)
    bb_re = k_re[..., None] * br - k_im[..., None] * bi
    bb_im = k_re[..., None] * bi + k_im[..., None] * br
    groups, nstate, nch = br.shape
    gpb = S5_GROUPS_PER_BLOCK
    nblk = groups // gpb
    eye = jnp.eye(gpb, dtype=F32)

    def in_block(bb):
        t = bb.reshape(nblk, gpb, nstate, nch).transpose(0, 1, 3, 2)
        t = t[:, :, :, None, :] * eye[None, :, None, :, None]
        return t.reshape(nblk, gpb * nch, gpb * nstate)

    def out_block(c):
        t = c.astype(F32).reshape(nblk, gpb, nch, nstate).transpose(0, 1, 3, 2)
        t = t[:, :, :, None, :] * eye[None, :, None, :, None]
        return t.reshape(nblk, gpb * nstate, gpb * nch)

    wb = jnp.concatenate([in_block(bb_re), in_block(bb_im)], axis=2).astype(BF16)
    wc = jnp.concatenate([out_block(c_re), -out_block(c_im)], axis=1).astype(BF16)
    a = jnp.stack([a_re.reshape(nblk, gpb * nstate), a_im.reshape(nblk, gpb * nstate)], axis=1)
    return wb, a, wc


def s5_mixer(h, norm_g, wt_in, lam_re, lam_im, log_step, b_re, b_im, c_re, c_im, d_skip, glu_w, glu_b):
    bsz, seqlen, d_model = h.shape
    width = glu_w.shape[0]
    tl = S5_TILE
    assert bsz == SUBLANES and seqlen % tl == 0
    wb, a, wc = _s5_weights(lam_re, lam_im, log_step, b_re, b_im, c_re, c_im)
    nstates = wb.shape[0] * wb.shape[2]
    consts = [norm_g.astype(F32).reshape(1, d_model), wt_in, wb, a, wc, d_skip.astype(F32).reshape(1, width),
              glu_w.astype(BF16), glu_b.astype(F32).reshape(1, width)]
    tok = lambda n: pl.BlockSpec((bsz, tl, n), lambda i: (0, i, 0))
    return pl.pallas_call(
        _s5_kernel, grid=(seqlen // tl,),
        in_specs=[tok(d_model)] + [_resident(c.shape) for c in consts],
        out_specs=tok(width),
        out_shape=jax.ShapeDtypeStruct((bsz, seqlen, width), BF16),
        scratch_shapes=[pltpu.VMEM((tl * bsz, nstates), F32), pltpu.VMEM((bsz, nstates), F32),
                        pltpu.VMEM((tl * bsz, width), F32), pltpu.VMEM((bsz, nstates), F32)],
        compiler_params=_params("arbitrary"), name="s5_mixer",
    )(h, *consts)


MIX_TILE = 256
MIX_HEADS = 4
MIX_QK = 128
MIX_V = 256
ROPE_BASE = 10000.0


def _rotary(t, cos, sin_signed):
    return t * cos + pltpu.roll(t, MIX_QK // 2, axis=1) * sin_signed


def _project_head(hn, wt_ref, p_ref, row0, g_row0, h):
    kw, vw = MIX_HEADS * MIX_QK, MIX_HEADS * MIX_V
    q_lo, k_lo, v_lo = h * MIX_QK, kw + h * MIX_QK, 2 * kw + h * MIX_V
    w_qk = jnp.concatenate([wt_ref[row0 + q_lo:row0 + q_lo + MIX_QK, :],
                            wt_ref[row0 + k_lo:row0 + k_lo + MIX_QK, :]], axis=0)
    qk = _dot_nt(hn, w_qk)
    p_ref[:, q_lo:q_lo + MIX_QK] = qk[:, :MIX_QK]
    p_ref[:, k_lo:k_lo + MIX_QK] = qk[:, MIX_QK:]
    p_ref[:, v_lo:v_lo + MIX_V] = _proj(hn, wt_ref, row0 + v_lo, row0 + v_lo + MIX_V)
    g_lo = h * MIX_V
    p_ref[:, 2 * kw + vw + g_lo:2 * kw + vw + g_lo + MIX_V] = _proj(hn, wt_ref, g_row0 + g_lo, g_row0 + g_lo + MIX_V)


def _ret_kernel(hn0_ref, hnext_ref, wt_ref, cos_ref, sin_ref, dmat_ref, qdec_ref, kdec_ref, cdec_ref,
                ng_ref, o_ref, state_ref, p_ref, *, row0, nt):
    step = pl.program_id(0)
    kw, vw = MIX_HEADS * MIX_QK, MIX_HEADS * MIX_V
    g_row0 = row0 + 2 * kw + vw

    @pl.when(step == 0)
    def _():
        for h in range(MIX_HEADS):
            _project_head(hn0_ref[...], wt_ref, p_ref, row0, g_row0, h)

    @pl.when(step % nt == 0)
    def _():
        state_ref[...] = jnp.zeros(state_ref.shape, F32)

    hn_next = hnext_ref[...]
    cos, sin = cos_ref[...], sin_ref[...]
    scale = MIX_QK ** -0.5
    for h in range(MIX_HEADS):
        vl = slice(h * MIX_V, (h + 1) * MIX_V)
        q = _rotary(p_ref[:, h * MIX_QK:(h + 1) * MIX_QK], cos, sin)
        k = _rotary(p_ref[:, kw + h * MIX_QK:kw + (h + 1) * MIX_QK], cos, sin) * scale
        v16 = p_ref[:, 2 * kw + h * MIX_V:2 * kw + (h + 1) * MIX_V].astype(BF16)
        gate = _silu(p_ref[:, 2 * kw + vw + h * MIX_V:2 * kw + vw + (h + 1) * MIX_V])
        q16 = q.astype(BF16)
        inner = _dot_nt(q16, k.astype(BF16)) * dmat_ref[h]
        state = state_ref[h]
        out = _dot(inner.astype(BF16), v16) + _dot_nt((q * qdec_ref[h]).astype(BF16), state.astype(BF16))
        state_ref[h] = state * cdec_ref[h] + _dot_tn(v16, (k * kdec_ref[h]).astype(BF16))
        _project_head(hn_next, wt_ref, p_ref, row0, g_row0, h)
        mu = jnp.mean(out, axis=-1, keepdims=True)
        cen = out - mu
        var = jnp.mean(cen * cen, axis=-1, keepdims=True)
        o = cen * lax.rsqrt(var + NORM_EPS) * ng_ref[:, vl]
        o_ref[:, vl] = (o * gate).astype(o_ref.dtype)


def _rope_tables(seqlen):
    half = MIX_QK // 2
    inv_freq = (np.float32(ROPE_BASE) ** (-np.arange(half, dtype=np.float32) / half)).astype(np.float32)
    ang = (np.arange(seqlen, dtype=np.float32)[:, None] * inv_freq[None, :]).astype(np.float64)
    cos, sin = np.cos(ang), np.sin(ang)
    return (jnp.asarray(np.concatenate([cos, cos], axis=1), F32),
            jnp.asarray(np.concatenate([-sin, sin], axis=1), F32))


def retention_mixer(hn, wt_in, row0, norm_g, batch, seqlen):
    c = MIX_TILE
    nt = seqlen // c
    d_model = hn.shape[1]
    kw, vw = MIX_HEADS * MIX_QK, MIX_HEADS * MIX_V
    assert seqlen % c == 0 and hn.shape[0] == batch * seqlen
    assert wt_in.shape[1] == d_model and row0 + 2 * kw + 2 * vw <= wt_in.shape[0]
    cos, sin = _rope_tables(seqlen)
    log_gamma = np.log(1.0 - 2.0 ** (-5.0 - np.arange(MIX_HEADS, dtype=np.float64)))
    pos = np.arange(c, dtype=np.float64)
    diff = pos[:, None] - pos[None, :]
    dmat = jnp.asarray(np.where(diff >= 0, np.exp(log_gamma[:, None, None] * np.maximum(diff, 0.0)), 0.0), F32)
    lanes = lambda col: jnp.asarray(np.broadcast_to(col[:, :, None], (MIX_HEADS, c, MIX_QK)), F32)
    qdec = lanes(np.exp(log_gamma[:, None] * (pos + 1.0)))
    kdec = lanes(np.exp(log_gamma[:, None] * (c - 1.0 - pos)))
    cdec = jnp.asarray(np.broadcast_to(np.exp(log_gamma * c)[:, None, None], (MIX_HEADS, 1, MIX_QK)), F32)
    consts = [dmat, qdec, kdec, cdec, norm_g.astype(F32).reshape(1, vw)]
    ntiles = batch * nt
    rope = pl.BlockSpec((c, MIX_QK), lambda s: (s % nt, 0))
    return pl.pallas_call(
        functools.partial(_ret_kernel, row0=row0, nt=nt), grid=(ntiles,),
        in_specs=[pl.BlockSpec((c, d_model), lambda s: (0, 0)),
                  pl.BlockSpec((c, d_model), lambda s: (jnp.minimum(s + 1, ntiles - 1), 0)),
                  _resident(wt_in.shape), rope, rope] + [_resident(x.shape) for x in consts],
        out_specs=pl.BlockSpec((c, vw), lambda s: (s, 0)),
        out_shape=jax.ShapeDtypeStruct((batch * seqlen, vw), BF16),
        scratch_shapes=[pltpu.VMEM((MIX_HEADS, MIX_V, MIX_QK), F32), pltpu.VMEM((c, 2 * kw + 2 * vw), F32)],
        compiler_params=_params("arbitrary"), name="retention_mixer",
    )(hn, hn, wt_in, cos, sin, *consts)


GLA_CHUNK = 64
GLA_TAU = 16.0


def _log_sigmoid(x):
    return jnp.minimum(x, 0.0) - jnp.log1p(jnp.exp(-jnp.abs(x)))


def _gla_kernel(hn0_ref, hnext_ref, wt_ref, gw_ref, gb_ref, ng_ref, cumsel_ref, lastsel_ref,
                o_ref, state_ref, p_ref, *, row0, rank, nt):
    tl = hnext_ref.shape[0]
    nchunk = tl // GLA_CHUNK
    kw, vw = MIX_HEADS * MIX_QK, MIX_HEADS * MIX_V
    step = pl.program_id(0)
    a_row0 = row0 + 2 * kw + vw
    g_row0 = a_row0 + rank
    a_col0 = 2 * kw + 2 * vw

    @pl.when(step == 0)
    def _():
        hn0 = hn0_ref[...]
        p_ref[:, a_col0:] = _proj_narrow(hn0, wt_ref, a_row0, a_row0 + rank)
        for h in range(MIX_HEADS):
            _project_head(hn0, wt_ref, p_ref, row0, g_row0, h)

    @pl.when(step % nt == 0)
    def _():
        state_ref[...] = jnp.zeros(state_ref.shape, F32)

    hn_next = hnext_ref[...]
    a_lr = p_ref[:, a_col0:]
    log_a = _log_sigmoid(_dot(a_lr.astype(BF16), gw_ref[...]) + gb_ref[...]) / GLA_TAU
    p_ref[:, a_col0:] = _proj_narrow(hn_next, wt_ref, a_row0, a_row0 + rank)
    bcum = _dot_select(cumsel_ref[...], log_a)
    blast = _dot_select(lastsel_ref[...], log_a)
    grow = jnp.exp(bcum)
    row = lax.broadcasted_iota(jnp.int32, (tl, tl), 0)
    col = lax.broadcasted_iota(jnp.int32, (tl, tl), 1)
    shift = GLA_CHUNK.bit_length() - 1
    keep = (row >= col) & (jnp.right_shift(row, shift) == jnp.right_shift(col, shift))
    col_chunk = jnp.right_shift(lax.broadcasted_iota(jnp.int32, (1, tl), 1), shift)
    scale = MIX_QK ** -0.5
    for h in range(MIX_HEADS):
        kl = slice(h * MIX_QK, (h + 1) * MIX_QK)
        vl = slice(h * MIX_V, (h + 1) * MIX_V)
        k = p_ref[:, kw + h * MIX_QK:kw + (h + 1) * MIX_QK]
        q_in = (p_ref[:, kl] * scale * grow[:, kl]).astype(BF16)
        k_in = (k * jnp.exp(-bcum[:, kl])).astype(BF16)
        k_out = (k * jnp.exp(blast[:, kl] - bcum[:, kl])).astype(BF16)
        v = p_ref[:, 2 * kw + h * MIX_V:2 * kw + (h + 1) * MIX_V]
        gate = _silu(p_ref[:, 2 * kw + vw + h * MIX_V:2 * kw + vw + (h + 1) * MIX_V])
        v16 = v.astype(BF16)
        att = jnp.where(keep, _dot_nt(q_in, k_in), 0.0)
        out = _dot(att.astype(BF16), v16)
        v_t = v.T
        stacked = jnp.concatenate(
            [jnp.where(col_chunk == c, v_t, 0.0) for c in range(nchunk)], axis=0).astype(BF16)
        kv = _dot(stacked, k_out)
        _project_head(hn_next, wt_ref, p_ref, row0, g_row0, h)
        state = state_ref[h]
        inter = []
        for c in range(nchunk):
            rows = slice(c * GLA_CHUNK, (c + 1) * GLA_CHUNK)
            inter.append(_dot_nt(q_in[rows, :], state.astype(BF16)))
            decay = jnp.exp(blast[c * GLA_CHUNK:c * GLA_CHUNK + 1, kl])
            state = state * decay + kv[c * MIX_V:(c + 1) * MIX_V, :]
        state_ref[h] = state
        out = out + jnp.concatenate(inter, axis=0)
        o = out * lax.rsqrt(jnp.mean(out * out, axis=-1, keepdims=True) + NORM_EPS) * ng_ref[:, vl]
        o_ref[:, vl] = (o * gate).astype(o_ref.dtype)


def gla_mixer(hn, wt_in, row0, gate_w, gate_b, norm_g, batch, seqlen):
    tl = MIX_TILE
    nt = seqlen // tl
    d_model = hn.shape[1]
    kw, vw = MIX_HEADS * MIX_QK, MIX_HEADS * MIX_V
    assert seqlen % tl == 0 and tl % GLA_CHUNK == 0 and hn.shape[0] == batch * seqlen
    rank = gate_w.shape[0]
    assert wt_in.shape[1] == d_model and row0 + 2 * kw + 2 * vw + rank <= wt_in.shape[0]
    gw = jnp.pad(gate_w.astype(F32), ((0, LANES - rank), (0, 0))).astype(BF16)
    t = np.arange(tl)
    same = (t[:, None] // GLA_CHUNK) == (t[None, :] // GLA_CHUNK)
    cumsel = jnp.asarray(same & (t[:, None] >= t[None, :]), BF16)
    lastsel = jnp.asarray(same, BF16)
    consts = [gw, gate_b.astype(F32).reshape(1, kw), norm_g.astype(F32).reshape(1, vw), cumsel, lastsel]
    ntiles = batch * nt
    return pl.pallas_call(
        functools.partial(_gla_kernel, row0=row0, rank=rank, nt=nt), grid=(ntiles,),
        in_specs=[pl.BlockSpec((tl, d_model), lambda s: (0, 0)),
                  pl.BlockSpec((tl, d_model), lambda s: (jnp.minimum(s + 1, ntiles - 1), 0)),
                  _resident(wt_in.shape)] + [_resident(x.shape) for x in consts],
        out_specs=pl.BlockSpec((tl, vw), lambda s: (s, 0)),
        out_shape=jax.ShapeDtypeStruct((batch * seqlen, vw), BF16),
        scratch_shapes=[pltpu.VMEM((MIX_HEADS, MIX_V, MIX_QK), F32),
                        pltpu.VMEM((tl, 2 * kw + 2 * vw + LANES), F32)],
        compiler_params=_params("arbitrary"), name="gla_mixer",
    )(hn, hn, wt_in, *consts)


def kernel(x, norm_mix_g, norm_ffn_g, final_norm_g, ev_in_w, s5_lam_re, s5_lam_im, s5_log_step, s5_b_re, s5_b_im, s5_c_re, s5_c_im, s5_d, s5_glu_w, s5_glu_b, ssd_conv_w, ssd_conv_b, ssd_dt_bias, ssd_a_log, ssd_d, ssd_norm_g, ev_out_w, od_in_w, ret_norm_g, gla_gate_w, gla_gate_b, gla_norm_g, od_out_w, ffn_gate_w, ffn_up_w, ffn_down_w):
    bsz, seqlen, d = x.shape
    rows = bsz * seqlen
    depth = norm_mix_g.shape[0]
    h = x.reshape(rows, d)
    hn = None
    wg, wu, wd = ffn_gate_w.astype(BF16), ffn_up_w.astype(BF16), ffn_down_w.astype(BF16)
    for layer in range(depth):
        i = layer // 2
        if layer % 2 == 0:
            s5_w = s5_glu_w.shape[1]
            wt_in = jnp.swapaxes(ev_in_w[i], 0, 1).astype(BF16)
            y_a = s5_mixer(h.reshape(bsz, seqlen, d), norm_mix_g[layer], wt_in, s5_lam_re[i], s5_lam_im[i],
                           s5_log_step[i], s5_b_re[i], s5_b_im[i], s5_c_re[i], s5_c_im[i], s5_d[i].reshape(-1),
                           s5_glu_w[i], s5_glu_b[i]).reshape(rows, s5_w)
            y_b = ssd_mixer(h, norm_mix_g[layer], wt_in, s5_w, ssd_conv_w[i], ssd_conv_b[i], ssd_dt_bias[i],
                            ssd_a_log[i], ssd_d[i], ssd_norm_g[i], bsz, seqlen)
            w_out = ev_out_w[i].astype(BF16)
        else:
            kw = gla_gate_w.shape[2]
            vw = ret_norm_g.shape[1]
            wt_in = jnp.swapaxes(od_in_w[i], 0, 1).astype(BF16)
            assert hn is not None, "odd layers follow an even layer"
            y_a = retention_mixer(hn, wt_in, 0, ret_norm_g[i], bsz, seqlen)
            y_b = gla_mixer(hn, wt_in, 2 * kw + 2 * vw, gla_gate_w[i], gla_gate_b[i], gla_norm_g[i], bsz, seqlen)
            w_out = od_out_w[i].astype(BF16)
        if layer == depth - 1:
            h = mix_ffn(h, y_a, y_b, w_out, norm_ffn_g[layer], wg, wu, wd, layer, final_norm_g, final_norm=True)
        else:
            h, hn = mix_ffn(h, y_a, y_b, w_out, norm_ffn_g[layer], wg, wu, wd, layer,
                            norm_mix_g[layer + 1], final_norm=False)
    return h.reshape(bsz, seqlen, d)
```

```python
import functools

import jax
import jax.numpy as jnp
import numpy as np
from jax import lax
from jax.experimental import pallas as pl
from jax.experimental.pallas import tpu as pltpu

F32 = jnp.float32
BF16 = jnp.bfloat16

NORM_EPS = 1e-6
VMEM_LIMIT_BYTES = 56 * 1024 * 1024
LANES = 128
SUBLANES = 8

ROW_TILE = 512
FFN_CHUNK = 256


def _rms(x, g):
    return x * lax.rsqrt(jnp.mean(x * x, axis=-1, keepdims=True) + NORM_EPS) * g


def _dot(a, b):
    return jnp.dot(a, b, preferred_element_type=F32)


def _resident(shape):
    zeros = (0,) * len(shape)
    return pl.BlockSpec(shape, lambda *_: zeros, pipeline_mode=pl.Buffered(1))


def _params(*semantics):
    return pltpu.CompilerParams(dimension_semantics=semantics,
                                vmem_limit_bytes=VMEM_LIMIT_BYTES)


def _mix_ffn_kernel(h_ref, ya_ref, yb_ref, wo_ref, g_ref, wg_ref, wu_ref, wd_ref,
                    ng_ref, *refs, final_norm):
    act_ref = refs[-1]
    na = ya_ref.shape[1]
    h1 = h_ref[...] + _dot(ya_ref[...], wo_ref[:na, :]) + _dot(yb_ref[...], wo_ref[na:, :])
    xn = _rms(h1, g_ref[...]).astype(BF16)
    hidden = wg_ref.shape[1]
    for j in range(hidden // FFN_CHUNK):
        cols = slice(j * FFN_CHUNK, (j + 1) * FFN_CHUNK)
        gate = _dot(xn, wg_ref[:, cols])
        up = _dot(xn, wu_ref[:, cols])
        act_ref[:, cols] = (gate * jax.nn.sigmoid(gate) * up).astype(BF16)
    out = h1 + _dot(act_ref[...], wd_ref[...])
    normed = _rms(out, ng_ref[...])
    if final_norm:
        refs[0][...] = normed
    else:
        refs[0][...] = out
        refs[1][...] = normed.astype(BF16)


def _resident_layer(shape, layer):
    tail = (0,) * (len(shape) - 1)
    return pl.BlockSpec((None,) + tuple(shape[1:]), lambda *_: (layer,) + tail, pipeline_mode=pl.Buffered(1))


def mix_ffn(h, ya, yb, wo, g, wg, wu, wd, layer, next_g, final_norm, row_tile=ROW_TILE):
    rows, d = h.shape
    hidden = wg.shape[2]
    assert rows % row_tile == 0 and hidden % FFN_CHUNK == 0
    assert wo.shape == (ya.shape[1] + yb.shape[1], d)
    row = lambda n: pl.BlockSpec((row_tile, n), lambda i: (i, 0))
    out_shape = [jax.ShapeDtypeStruct((rows, d), F32)]
    if not final_norm:
        out_shape.append(jax.ShapeDtypeStruct((rows, d), BF16))
    outs = pl.pallas_call(
        functools.partial(_mix_ffn_kernel, final_norm=final_norm),
        grid=(rows // row_tile,),
        in_specs=[row(d), row(ya.shape[1]), row(yb.shape[1]), _resident(wo.shape), _resident((1, d)),
                  _resident_layer(wg.shape, layer), _resident_layer(wu.shape, layer),
                  _resident_layer(wd.shape, layer), _resident((1, d))],
        out_specs=[row(d)] * len(out_shape),
        out_shape=out_shape,
        scratch_shapes=[pltpu.VMEM((row_tile, hidden), BF16)],
        compiler_params=_params("parallel"), name="mix_ffn",
    )(h, ya, yb, wo, g.reshape(1, d), wg, wu, wd, next_g.reshape(1, d))
    return outs[0] if final_norm else tuple(outs)


def _split3(a):
    hi = a.astype(BF16)
    r1 = a - hi.astype(F32)
    mid = r1.astype(BF16)
    lo = (r1 - mid.astype(F32)).astype(BF16)
    return hi, mid, lo


PACK = 16


def _pack3(a):
    hi, mid, lo = _split3(a)
    packed = hi.astype(F32) + pltpu.roll(mid.astype(F32), PACK, 1) + pltpu.roll(lo.astype(F32), 2 * PACK, 1)
    return packed.astype(BF16)


def _unpack3(r):
    return r + pltpu.roll(r, LANES - PACK, 1) + pltpu.roll(r, LANES - 2 * PACK, 1)


def _dot_nt(a, b):
    return lax.dot_general(a, b, (((1,), (1,)), ((), ())), preferred_element_type=F32)


def _dot_tn(a, b):
    return lax.dot_general(a, b, (((0,), (0,)), ((), ())), preferred_element_type=F32)


def _proj(hn, wt_ref, lo, hi):
    return _dot_nt(hn, wt_ref[lo:hi, :])


def _proj_narrow(hn, wt_ref, lo, hi):
    pad = jnp.zeros((LANES - (hi - lo), wt_ref.shape[1]), wt_ref.dtype)
    return _dot_nt(hn, jnp.concatenate([wt_ref[lo:hi, :], pad], axis=0))


def _silu(x):
    return x * jax.nn.sigmoid(x)


def _softplus(x):
    return jnp.maximum(x, 0.0) + jnp.log1p(jnp.exp(-jnp.abs(x)))


SSD_TILE = 256
SSD_HEADS = 16
SSD_HEADDIM = 64
SSD_GROUPS = 4
SSD_STATE = 64
SSD_CONV = 4
PROJ_PIECE = 256


def _proj_pieces(lo, hi, narrow_rows):
    pieces = [(a, min(a + PROJ_PIECE, hi), False) for a in range(lo, hi, PROJ_PIECE)]
    return pieces + [(hi, hi + narrow_rows, True)]


def _ssd_kernel(h0_ref, hnext_ref, hg_ref, wt_ref, cw_ref, cb_ref, dtb_ref, alog_ref, dsk_ref, ng_ref,
                expand_ref, tril_ref, hmask_ref, gmask_ref, bdmask_ref,
                o_ref, xpad_ref, state_ref, lhs_ref, rhs_ref, p_ref, z_ref, *, row0, nt):
    cs = hnext_ref.shape[0]
    inner = o_ref.shape[1]
    gw = inner // SSD_GROUPS
    hpg = SSD_HEADS // SSD_GROUPS
    nslab = xpad_ref.shape[0]
    conv_dim = nslab * LANES
    step = pl.program_id(0)

    def project(h_tile, pieces):
        hn = _rms(h_tile, hg_ref[...]).astype(BF16)

        def emit(count=1):
            for _ in range(min(count, len(pieces))):
                lo, hi, narrow = pieces.pop(0)
                col = lo - row0
                if narrow:
                    p_ref[:, col:col + LANES] = _proj_narrow(hn, wt_ref, lo, hi)
                else:
                    p_ref[:, col:col + hi - lo] = _proj(hn, wt_ref, lo, hi)
        return emit

    all_pieces = lambda: _proj_pieces(row0, row0 + inner + conv_dim, SSD_HEADS)

    @pl.when(step == 0)
    def _():
        project(h0_ref[...], all_pieces())(len(all_pieces()))

    @pl.when(step % nt == 0)
    def _():
        xpad_ref[:, 0:8, :] = jnp.zeros((nslab, 8, LANES), F32)
        state_ref[...] = jnp.zeros(state_ref.shape, F32)

    dt_raw = p_ref[:, inner + conv_dim:]
    z_ref[...] = p_ref[:, :inner]
    for j in range(nslab):
        xpad_ref[j, 8:8 + cs, :] = p_ref[:, inner + j * LANES:inner + (j + 1) * LANES]
    emit_next = project(hnext_ref[...], all_pieces())

    convs = []
    for j in range(nslab):
        lanes = slice(j * LANES, (j + 1) * LANES)
        conv = cb_ref[:, lanes]
        for k in range(SSD_CONV):
            lo = 8 - (SSD_CONV - 1) + k
            conv = conv + cw_ref[k:k + 1, lanes] * xpad_ref[j, lo:lo + cs, :]
        xpad_ref[j, 0:8, :] = xpad_ref[j, cs:cs + 8, :]
        convs.append(conv)
        if j % 3 == 2:
            emit_next()
    xc = _silu(jnp.concatenate(convs, axis=1))
    xs = xc[:, :inner]
    bm = xc[:, inner:inner + SSD_GROUPS * SSD_STATE]
    cm = xc[:, inner + SSD_GROUPS * SSD_STATE:]

    head_lane = lax.broadcasted_iota(jnp.int32, (cs, LANES), 1) < SSD_HEADS
    dt = jnp.where(head_lane, _softplus(dt_raw + dtb_ref[...]), 0.0)
    da = dt * -jnp.exp(alog_ref[...])
    a_cum = jnp.where(head_lane, _unpack3(_dot(tril_ref[...], _pack3(da))), 0.0)
    a_cum_t = a_cum.T
    expand = expand_ref[...]
    xdt = xs * _dot(_pack3(dt), expand)
    a_x = _dot(_pack3(a_cum), expand)
    a_last = a_x[cs - 1:cs, :]
    decay_in = jnp.exp(a_x)
    xd = (xdt * jnp.exp(a_last - a_x)).astype(BF16)
    xdt = xdt.astype(BF16)

    state = state_ref[...]
    y = _dot(cm.astype(BF16), state.astype(BF16)) * decay_in
    new_state = state * jnp.exp(a_last) + _dot(bm.T.astype(BF16), xd)
    state_ref[...] = new_state * bdmask_ref[...]

    row = lax.broadcasted_iota(jnp.int32, (cs, cs), 0)
    col = lax.broadcasted_iota(jnp.int32, (cs, cs), 1)
    causal = row >= col
    bm16 = bm.astype(BF16)
    ys = []
    for g in range(SSD_GROUPS):
        cb = _dot_nt((cm * gmask_ref[g:g + 1, :]).astype(BF16), bm16)
        xg = xdt[:, g * gw:(g + 1) * gw]
        for j in range(hpg):
            h = g * hpg + j
            diff = a_cum[:, h:h + 1] - a_cum_t[h:h + 1, :]
            seg = jnp.exp(jnp.where(causal, diff, -jnp.inf))
            lhs_ref[:, j * cs:(j + 1) * cs] = (cb * seg).astype(BF16)
            rhs_ref[j * cs:(j + 1) * cs, :] = xg * hmask_ref[j:j + 1, :]
        ys.append(_dot(lhs_ref[...], rhs_ref[...]))
        emit_next(2)
    y = y + jnp.concatenate(ys, axis=1) + dsk_ref[...] * xs
    y = y * _silu(z_ref[...])
    o_ref[...] = _rms(y, ng_ref[...]).astype(o_ref.dtype)
    emit_next(len(all_pieces()))


def ssd_mixer(h, h_norm_g, wt_in, row0, conv_w, conv_b, dt_bias, a_log, d_skip, norm_g, batch, seqlen):
    cs = SSD_TILE
    d_model = h.shape[1]
    inner = SSD_HEADS * SSD_HEADDIM
    conv_dim = conv_w.shape[1]
    gw = inner // SSD_GROUPS
    gs = SSD_GROUPS * SSD_STATE
    hpg = SSD_HEADS // SSD_GROUPS
    nt = seqlen // cs
    assert seqlen % cs == 0 and h.shape[0] == batch * seqlen and conv_dim % LANES == 0
    assert wt_in.shape[1] == d_model and row0 + inner + conv_dim + SSD_HEADS <= wt_in.shape[0]

    pad = lambda v: jnp.pad(v.astype(F32), (0, LANES - SSD_HEADS)).reshape(1, LANES)
    assert SSD_HEADS == PACK
    lane_head = np.arange(inner) // SSD_HEADDIM
    packed_head = np.where(np.arange(LANES) < 3 * PACK, np.arange(LANES) % PACK, -1)
    expand = jnp.asarray(packed_head[:, None] == lane_head[None, :], BF16)
    tril = jnp.asarray(np.arange(cs)[:, None] >= np.arange(cs)[None, :], BF16)
    hmask = jnp.asarray(np.arange(hpg)[:, None] == (np.arange(gw) // SSD_HEADDIM)[None, :], BF16)
    gmask = jnp.asarray(np.arange(SSD_GROUPS)[:, None] == (np.arange(gs) // SSD_STATE)[None, :], F32)
    bdmask = jnp.asarray((np.arange(gs) // SSD_STATE)[:, None] == (np.arange(inner) // gw)[None, :], F32)

    ntiles = batch * nt
    consts = [h_norm_g.astype(F32).reshape(1, d_model), wt_in,
              conv_w.astype(F32), conv_b.astype(F32).reshape(1, -1), pad(dt_bias), pad(a_log),
              jnp.repeat(d_skip.astype(F32), SSD_HEADDIM).reshape(1, inner), norm_g.astype(F32).reshape(1, inner),
              expand, tril, hmask, gmask, bdmask]
    return pl.pallas_call(
        functools.partial(_ssd_kernel, row0=row0, nt=nt), grid=(ntiles,),
        in_specs=[pl.BlockSpec((cs, d_model), lambda s: (0, 0)),
                  pl.BlockSpec((cs, d_model), lambda s: (jnp.minimum(s + 1, ntiles - 1), 0))]
        + [_resident(c.shape) for c in consts],
        out_specs=pl.BlockSpec((cs, inner), lambda s: (s, 0)),
        out_shape=jax.ShapeDtypeStruct((batch * seqlen, inner), BF16),
        scratch_shapes=[pltpu.VMEM((conv_dim // LANES, cs + 8, LANES), F32), pltpu.VMEM((gs, inner), F32),
                        pltpu.VMEM((cs, hpg * cs), BF16), pltpu.VMEM((hpg * cs, gw), BF16),
                        pltpu.VMEM((cs, inner + conv_dim + LANES), F32), pltpu.VMEM((cs, inner), F32)],
        compiler_params=_params("arbitrary"), name="ssd_mixer",
    )(h, h, *consts)


S5_TILE = 128
S5_CHUNK = 32
S5_GROUP = 16
S5_STATE = 64
S5_GROUPS_PER_BLOCK = 16


def _s5_kernel(h_ref, ng_ref, wt_ref, wb_ref, a_ref, wc_ref, d_ref, gw_ref, gb_ref, o_ref,
               bu_ref, x_ref, y_ref, abc_ref):
    bsz, tl, d_model = h_ref.shape
    width = d_ref.shape[1]
    nblk = wb_ref.shape[0]
    kw = wb_ref.shape[1]
    sw = a_ref.shape[2]
    rows = tl * bsz

    lanes = lambda kb: slice(kb * 2 * sw, (kb + 1) * 2 * sw)
    re = lambda kb: slice(kb * 2 * sw, kb * 2 * sw + sw)
    im = lambda kb: slice(kb * 2 * sw + sw, (kb + 1) * 2 * sw)

    @pl.when(pl.program_id(0) == 0)
    def _():
        x_ref[...] = jnp.zeros(x_ref.shape, F32)
        for kb in range(nblk):
            abc_ref[:, re(kb)] = jnp.broadcast_to(a_ref[kb, 0:1, :], (bsz, sw))
            abc_ref[:, im(kb)] = jnp.broadcast_to(a_ref[kb, 1:2, :], (bsz, sw))

    hn = _rms(h_ref[...].reshape(rows, d_model), ng_ref[...]).astype(BF16)
    u = _proj(hn, wt_ref, 0, width)
    u = jnp.swapaxes(u.reshape(bsz, tl, width), 0, 1).reshape(rows, width)
    u16 = u.astype(BF16)

    crows = S5_CHUNK * bsz
    nchunk = tl // S5_CHUNK

    def input_stage(j):
        rows_j = slice(j * crows, (j + 1) * crows)
        for kb in range(nblk):
            bu_ref[rows_j, lanes(kb)] = _dot(u16[rows_j, kb * kw:(kb + 1) * kw], wb_ref[kb])

    def scan_stage(j, state):
        for t in range(j * S5_CHUNK, (j + 1) * S5_CHUNK):
            r = slice(t * bsz, (t + 1) * bsz)
            new_state = []
            for kb in range(nblk):
                x_re, x_im = state[kb]
                a_re, a_im = abc_ref[:, re(kb)], abc_ref[:, im(kb)]
                n_re = a_re * x_re - a_im * x_im + bu_ref[r, re(kb)]
                n_im = a_re * x_im + a_im * x_re + bu_ref[r, im(kb)]
                bu_ref[r, re(kb)] = n_re
                bu_ref[r, im(kb)] = n_im
                new_state.append((n_re, n_im))
            state = new_state
        return state

    def output_stage(j):
        rows_j = slice(j * crows, (j + 1) * crows)
        ys = [_dot(bu_ref[rows_j, lanes(kb)].astype(BF16), wc_ref[kb]) for kb in range(nblk)]
        y = jnp.concatenate(ys, axis=1) + d_ref[...] * u[rows_j, :]
        z = jax.nn.gelu(y)
        y_ref[rows_j, :] = z * jax.nn.sigmoid(_dot(z.astype(BF16), gw_ref[...]) + gb_ref[...])

    state = [(x_ref[:, re(kb)], x_ref[:, im(kb)]) for kb in range(nblk)]
    input_stage(0)
    for j in range(nchunk):
        if j + 1 < nchunk:
            input_stage(j + 1)
        state = scan_stage(j, state)
        output_stage(j)
    for kb in range(nblk):
        x_ref[:, re(kb)] = state[kb][0]
        x_ref[:, im(kb)] = state[kb][1]
    o_ref[...] = jnp.swapaxes(y_ref[...].reshape(tl, bsz, width), 0, 1).astype(o_ref.dtype)


def _s5_weights(lam_re, lam_im, log_step, b_re, b_im, c_re, c_im):
    lr, li = lam_re.astype(F32), lam_im.astype(F32)
    step = jnp.exp(log_step.astype(F32))[:, None]
    mag = jnp.exp(lr * step)
    a_re, a_im = mag * jnp.cos(li * step), mag * jnp.sin(li * step)
    den = lr * lr + li * li
    k_re = ((a_re - 1.0) * lr + a_im * li) / den
    k_im = (a_im * lr - (a_re - 1.0) * li) / den
    br, bi = b_re.astype(F32), b_im.astype(F32)
    bb_re = k_re[..., None] * br - k_im[..., None] * bi
    bb_im = k_re[..., None] * bi + k_im[..., None] * br
    groups, nstate, nch = br.shape
    gpb = S5_GROUPS_PER_BLOCK
    nblk = groups // gpb
    eye = jnp.eye(gpb, dtype=F32)

    def in_block(bb):
        t = bb.reshape(nblk, gpb, nstate, nch).transpose(0, 1, 3, 2)
        t = t[:, :, :, None, :] * eye[None, :, None, :, None]
        return t.reshape(nblk, gpb * nch, gpb * nstate)

    def out_block(c):
        t = c.astype(F32).reshape(nblk, gpb, nch, nstate).transpose(0, 1, 3, 2)
        t = t[:, :, :, None, :] * eye[None, :, None, :, None]
        return t.reshape(nblk, gpb * nstate, gpb * nch)

    wb = jnp.concatenate([in_block(bb_re), in_block(bb_im)], axis=2).astype(BF16)
    wc = jnp.concatenate([out_block(c_re), -out_block(c_im)], axis=1).astype(BF16)
    a = jnp.stack([a_re.reshape(nblk, gpb * nstate), a_im.reshape(nblk, gpb * nstate)], axis=1)
    return wb, a, wc


def s5_mixer(h, norm_g, wt_in, lam_re, lam_im, log_step, b_re, b_im, c_re, c_im, d_skip, glu_w, glu_b):
    bsz, seqlen, d_model = h.shape
    width = glu_w.shape[0]
    tl = S5_TILE
    assert bsz == SUBLANES and seqlen % tl == 0
    wb, a, wc = _s5_weights(lam_re, lam_im, log_step, b_re, b_im, c_re, c_im)
    nstates = wb.shape[0] * wb.shape[2]
    consts = [norm_g.astype(F32).reshape(1, d_model), wt_in, wb, a, wc, d_skip.astype(F32).reshape(1, width),
              glu_w.astype(BF16), glu_b.astype(F32).reshape(1, width)]
    tok = lambda n: pl.BlockSpec((bsz, tl, n), lambda i: (0, i, 0))
    return pl.pallas_call(
        _s5_kernel, grid=(seqlen // tl,),
        in_specs=[tok(d_model)] + [_resident(c.shape) for c in consts],
        out_specs=tok(width),
        out_shape=jax.ShapeDtypeStruct((bsz, seqlen, width), BF16),
        scratch_shapes=[pltpu.VMEM((tl * bsz, nstates), F32), pltpu.VMEM((bsz, nstates), F32),
                        pltpu.VMEM((tl * bsz, width), F32), pltpu.VMEM((bsz, nstates), F32)],
        compiler_params=_params("arbitrary"), name="s5_mixer",
    )(h, *consts)


MIX_TILE = 256
MIX_HEADS = 4
MIX_QK = 128
MIX_V = 256
ROPE_BASE = 10000.0


def _rotary(t, cos, sin_signed):
    return t * cos + pltpu.roll(t, MIX_QK // 2, axis=1) * sin_signed


def _project_head(hn, wt_ref, p_ref, row0, g_row0, h):
    kw, vw = MIX_HEADS * MIX_QK, MIX_HEADS * MIX_V
    q_lo, k_lo, v_lo = h * MIX_QK, kw + h * MIX_QK, 2 * kw + h * MIX_V
    w_qk = jnp.concatenate([wt_ref[row0 + q_lo:row0 + q_lo + MIX_QK, :],
                            wt_ref[row0 + k_lo:row0 + k_lo + MIX_QK, :]], axis=0)
    qk = _dot_nt(hn, w_qk)
    p_ref[:, q_lo:q_lo + MIX_QK] = qk[:, :MIX_QK]
    p_ref[:, k_lo:k_lo + MIX_QK] = qk[:, MIX_QK:]
    p_ref[:, v_lo:v_lo + MIX_V] = _proj(hn, wt_ref, row0 + v_lo, row0 + v_lo + MIX_V)
    g_lo = h * MIX_V
    p_ref[:, 2 * kw + vw + g_lo:2 * kw + vw + g_lo + MIX_V] = _proj(hn, wt_ref, g_row0 + g_lo, g_row0 + g_lo + MIX_V)


def _ret_kernel(hn0_ref, hnext_ref, wt_ref, cos_ref, sin_ref, dmat_ref, qdec_ref, kdec_ref, cdec_ref,
                ng_ref, o_ref, state_ref, p_ref, *, row0, nt):
    step = pl.program_id(0)
    kw, vw = MIX_HEADS * MIX_QK, MIX_HEADS * MIX_V
    g_row0 = row0 + 2 * kw + vw

    @pl.when(step == 0)
    def _():
        for h in range(MIX_HEADS):
            _project_head(hn0_ref[...], wt_ref, p_ref, row0, g_row0, h)

    @pl.when(step % nt == 0)
    def _():
        state_ref[...] = jnp.zeros(state_ref.shape, F32)

    hn_next = hnext_ref[...]
    cos, sin = cos_ref[...], sin_ref[...]
    scale = MIX_QK ** -0.5
    for h in range(MIX_HEADS):
        vl = slice(h * MIX_V, (h + 1) * MIX_V)
        q = _rotary(p_ref[:, h * MIX_QK:(h + 1) * MIX_QK], cos, sin)
        k = _rotary(p_ref[:, kw + h * MIX_QK:kw + (h + 1) * MIX_QK], cos, sin) * scale
        v16 = p_ref[:, 2 * kw + h * MIX_V:2 * kw + (h + 1) * MIX_V].astype(BF16)
        gate = _silu(p_ref[:, 2 * kw + vw + h * MIX_V:2 * kw + vw + (h + 1) * MIX_V])
        q16 = q.astype(BF16)
        inner = _dot_nt(q16, k.astype(BF16)) * dmat_ref[h]
        state = state_ref[h]
        out = _dot(inner.astype(BF16), v16) + _dot_nt((q * qdec_ref[h]).astype(BF16), state.astype(BF16))
        state_ref[h] = state * cdec_ref[h] + _dot_tn(v16, (k * kdec_ref[h]).astype(BF16))
        _project_head(hn_next, wt_ref, p_ref, row0, g_row0, h)
        mu = jnp.mean(out, axis=-1, keepdims=True)
        cen = out - mu
        var = jnp.mean(cen * cen, axis=-1, keepdims=True)
        o = cen * lax.rsqrt(var + NORM_EPS) * ng_ref[:, vl]
        o_ref[:, vl] = (o * gate).astype(o_ref.dtype)


def _rope_tables(seqlen):
    half = MIX_QK // 2
    inv_freq = (np.float32(ROPE_BASE) ** (-np.arange(half, dtype=np.float32) / half)).astype(np.float32)
    ang = (np.arange(seqlen, dtype=np.float32)[:, None] * inv_freq[None, :]).astype(np.float64)
    cos, sin = np.cos(ang), np.sin(ang)
    return (jnp.asarray(np.concatenate([cos, cos], axis=1), F32),
            jnp.asarray(np.concatenate([-sin, sin], axis=1), F32))


def retention_mixer(hn, wt_in, row0, norm_g, batch, seqlen):
    c = MIX_TILE
    nt = seqlen // c
    d_model = hn.shape[1]
    kw, vw = MIX_HEADS * MIX_QK, MIX_HEADS * MIX_V
    assert seqlen % c == 0 and hn.shape[0] == batch * seqlen
    assert wt_in.shape[1] == d_model and row0 + 2 * kw + 2 * vw <= wt_in.shape[0]
    cos, sin = _rope_tables(seqlen)
    log_gamma = np.log(1.0 - 2.0 ** (-5.0 - np.arange(MIX_HEADS, dtype=np.float64)))
    pos = np.arange(c, dtype=np.float64)
    diff = pos[:, None] - pos[None, :]
    dmat = jnp.asarray(np.where(diff >= 0, np.exp(log_gamma[:, None, None] * np.maximum(diff, 0.0)), 0.0), F32)
    lanes = lambda col: jnp.asarray(np.broadcast_to(col[:, :, None], (MIX_HEADS, c, MIX_QK)), F32)
    qdec = lanes(np.exp(log_gamma[:, None] * (pos + 1.0)))
    kdec = lanes(np.exp(log_gamma[:, None] * (c - 1.0 - pos)))
    cdec = jnp.asarray(np.broadcast_to(np.exp(log_gamma * c)[:, None, None], (MIX_HEADS, 1, MIX_QK)), F32)
    consts = [dmat, qdec, kdec, cdec, norm_g.astype(F32).reshape(1, vw)]
    ntiles = batch * nt
    rope = pl.BlockSpec((c, MIX_QK), lambda s: (s % nt, 0))
    return pl.pallas_call(
        functools.partial(_ret_kernel, row0=row0, nt=nt), grid=(ntiles,),
        in_specs=[pl.BlockSpec((c, d_model), lambda s: (0, 0)),
                  pl.BlockSpec((c, d_model), lambda s: (jnp.minimum(s + 1, ntiles - 1), 0)),
                  _resident(wt_in.shape), rope, rope] + [_resident(x.shape) for x in consts],
        out_specs=pl.BlockSpec((c, vw), lambda s: (s, 0)),
        out_shape=jax.ShapeDtypeStruct((batch * seqlen, vw), BF16),
        scratch_shapes=[pltpu.VMEM((MIX_HEADS, MIX_V, MIX_QK), F32), pltpu.VMEM((c, 2 * kw + 2 * vw), F32)],
        compiler_params=_params("arbitrary"), name="retention_mixer",
    )(hn, hn, wt_in, cos, sin, *consts)


GLA_CHUNK = 64
GLA_TAU = 16.0


def _log_sigmoid(x):
    return jnp.minimum(x, 0.0) - jnp.log1p(jnp.exp(-jnp.abs(x)))


def _gla_kernel(hn0_ref, hnext_ref, wt_ref, gw_ref, gb_ref, ng_ref, cumsel_ref,
                o_ref, state_ref, p_ref, *, row0, rank, nt):
    tl = hnext_ref.shape[0]
    nchunk = tl // GLA_CHUNK
    kw, vw = MIX_HEADS * MIX_QK, MIX_HEADS * MIX_V
    step = pl.program_id(0)
    a_row0 = row0 + 2 * kw + vw
    g_row0 = a_row0 + rank
    a_col0 = 2 * kw + 2 * vw

    @pl.when(step == 0)
    def _():
        hn0 = hn0_ref[...]
        p_ref[:, a_col0:] = _proj_narrow(hn0, wt_ref, a_row0, a_row0 + rank)
        for h in range(MIX_HEADS):
            _project_head(hn0, wt_ref, p_ref, row0, g_row0, h)

    @pl.when(step % nt == 0)
    def _():
        state_ref[...] = jnp.zeros(state_ref.shape, F32)

    hn_next = hnext_ref[...]
    a_lr = p_ref[:, a_col0:]
    log_a = _log_sigmoid(_dot(a_lr.astype(BF16), gw_ref[...]) + gb_ref[...]) / GLA_TAU
    p_ref[:, a_col0:] = _proj_narrow(hn_next, wt_ref, a_row0, a_row0 + rank)
    hi, mid, lo = _split3(log_a)
    zeros = jnp.zeros((GLA_CHUNK, kw), BF16)
    bcums, blasts = [], []
    for c in range(nchunk):
        rows = slice(c * GLA_CHUNK, (c + 1) * GLA_CHUNK)
        terms = jnp.concatenate([hi[rows, :], mid[rows, :], lo[rows, :], zeros], axis=0)
        bc = _dot(cumsel_ref[...], terms)
        bcums.append(bc)
        blasts.append(jnp.broadcast_to(bc[GLA_CHUNK - 1:GLA_CHUNK, :], (GLA_CHUNK, kw)))
    bcum = jnp.concatenate(bcums, axis=0)
    blast = jnp.concatenate(blasts, axis=0)
    grow = jnp.exp(bcum)
    row = lax.broadcasted_iota(jnp.int32, (tl, tl), 0)
    col = lax.broadcasted_iota(jnp.int32, (tl, tl), 1)
    shift = GLA_CHUNK.bit_length() - 1
    keep = (row >= col) & (jnp.right_shift(row, shift) == jnp.right_shift(col, shift))
    bcum_t = bcum.T
    scale = MIX_QK ** -0.5
    for h in range(MIX_HEADS):
        kl = slice(h * MIX_QK, (h + 1) * MIX_QK)
        vl = slice(h * MIX_V, (h + 1) * MIX_V)
        k = p_ref[:, kw + h * MIX_QK:kw + (h + 1) * MIX_QK]
        q_in = (p_ref[:, kl] * scale * grow[:, kl]).astype(BF16)
        k_in = (k * jnp.exp(-bcum[:, kl])).astype(BF16)
        k_out = (k * jnp.exp(blast[:, kl] - bcum[:, kl])).astype(BF16)
        v = p_ref[:, 2 * kw + h * MIX_V:2 * kw + (h + 1) * MIX_V]
        gate = _silu(p_ref[:, 2 * kw + vw + h * MIX_V:2 * kw + vw + (h + 1) * MIX_V])
        v16 = v.astype(BF16)
        att = jnp.where(keep, _dot_nt(q_in, k_in), 0.0)
        out = _dot(att.astype(BF16), v16)
        _project_head(hn_next, wt_ref, p_ref, row0, g_row0, h)
        chunks = [slice(c * GLA_CHUNK, (c + 1) * GLA_CHUNK) for c in range(nchunk)]
        kvs = [_dot_tn(k_out[rows, :], v16[rows, :]) for rows in chunks]
        decays = [jnp.exp(bcum_t[h * MIX_QK:(h + 1) * MIX_QK, rows.stop - 1:rows.stop]) for rows in chunks]
        state = state_ref[h]
        inter = []
        for rows, kv, decay in zip(chunks, kvs, decays):
            inter.append(_dot(q_in[rows, :], state.astype(BF16)))
            state = state * decay + kv
        state_ref[h] = state
        out = out + jnp.concatenate(inter, axis=0)
        o = out * lax.rsqrt(jnp.mean(out * out, axis=-1, keepdims=True) + NORM_EPS) * ng_ref[:, vl]
        o_ref[:, vl] = (o * gate).astype(o_ref.dtype)


def gla_mixer(hn, wt_in, row0, gate_w, gate_b, norm_g, batch, seqlen):
    tl = MIX_TILE
    nt = seqlen // tl
    d_model = hn.shape[1]
    kw, vw = MIX_HEADS * MIX_QK, MIX_HEADS * MIX_V
    assert seqlen % tl == 0 and tl % GLA_CHUNK == 0 and hn.shape[0] == batch * seqlen
    rank = gate_w.shape[0]
    assert wt_in.shape[1] == d_model and row0 + 2 * kw + 2 * vw + rank <= wt_in.shape[0]
    gw = jnp.pad(gate_w.astype(F32), ((0, LANES - rank), (0, 0))).astype(BF16)
    t = np.arange(GLA_CHUNK)
    tril = t[:, None] >= t[None, :]
    cumsel = jnp.asarray(np.concatenate([tril, tril, tril, np.zeros_like(tril)], axis=1), BF16)
    consts = [gw, gate_b.astype(F32).reshape(1, kw), norm_g.astype(F32).reshape(1, vw), cumsel]
    ntiles = batch * nt
    return pl.pallas_call(
        functools.partial(_gla_kernel, row0=row0, rank=rank, nt=nt), grid=(ntiles,),
        in_specs=[pl.BlockSpec((tl, d_model), lambda s: (0, 0)),
                  pl.BlockSpec((tl, d_model), lambda s: (jnp.minimum(s + 1, ntiles - 1), 0)),
                  _resident(wt_in.shape)] + [_resident(x.shape) for x in consts],
        out_specs=pl.BlockSpec((tl, vw), lambda s: (s, 0)),
        out_shape=jax.ShapeDtypeStruct((batch * seqlen, vw), BF16),
        scratch_shapes=[pltpu.VMEM((MIX_HEADS, MIX_QK, MIX_V), F32),
                        pltpu.VMEM((tl, 2 * kw + 2 * vw + LANES), F32)],
        compiler_params=_params("arbitrary"), name="gla_mixer",
    )(hn, hn, wt_in, *consts)


def kernel(x, norm_mix_g, norm_ffn_g, final_norm_g, ev_in_w, s5_lam_re, s5_lam_im, s5_log_step, s5_b_re, s5_b_im, s5_c_re, s5_c_im, s5_d, s5_glu_w, s5_glu_b, ssd_conv_w, ssd_conv_b, ssd_dt_bias, ssd_a_log, ssd_d, ssd_norm_g, ev_out_w, od_in_w, ret_norm_g, gla_gate_w, gla_gate_b, gla_norm_g, od_out_w, ffn_gate_w, ffn_up_w, ffn_down_w):
    bsz, seqlen, d = x.shape
    rows = bsz * seqlen
    depth = norm_mix_g.shape[0]
    h = x.reshape(rows, d)
    hn = None
    wg, wu, wd = ffn_gate_w.astype(BF16), ffn_up_w.astype(BF16), ffn_down_w.astype(BF16)
    for layer in range(depth):
        i = layer // 2
        if layer % 2 == 0:
            s5_w = s5_glu_w.shape[1]
            wt_in = jnp.swapaxes(ev_in_w[i], 0, 1).astype(BF16)
            y_a = s5_mixer(h.reshape(bsz, seqlen, d), norm_mix_g[layer], wt_in, s5_lam_re[i], s5_lam_im[i],
                           s5_log_step[i], s5_b_re[i], s5_b_im[i], s5_c_re[i], s5_c_im[i], s5_d[i].reshape(-1),
                           s5_glu_w[i], s5_glu_b[i]).reshape(rows, s5_w)
            y_b = ssd_mixer(h, norm_mix_g[layer], wt_in, s5_w, ssd_conv_w[i], ssd_conv_b[i], ssd_dt_bias[i],
                            ssd_a_log[i], ssd_d[i], ssd_norm_g[i], bsz, seqlen)
            w_out = ev_out_w[i].astype(BF16)
        else:
            kw = gla_gate_w.shape[2]
            vw = ret_norm_g.shape[1]
            wt_in = jnp.swapaxes(od_in_w[i], 0, 1).astype(BF16)
            assert hn is not None, "odd layers follow an even layer"
            y_a = retention_mixer(hn, wt_in, 0, ret_norm_g[i], bsz, seqlen)
            y_b = gla_mixer(hn, wt_in, 2 * kw + 2 * vw, gla_gate_w[i], gla_gate_b[i], gla_norm_g[i], bsz, seqlen)
            w_out = od_out_w[i].astype(BF16)
        if layer == depth - 1:
            h = mix_ffn(h, y_a, y_b, w_out, norm_ffn_g[layer], wg, wu, wd, layer, final_norm_g, final_norm=True)
        else:
            h, hn = mix_ffn(h, y_a, y_b, w_out, norm_ffn_g[layer], wg, wu, wd, layer,
                            norm_mix_g[layer + 1], final_norm=False)
    return h.reshape(bsz, seqlen, d)
```

```python
import functools

import jax
import jax.numpy as jnp
import numpy as np
from jax import lax
from jax.experimental import pallas as pl
from jax.experimental.pallas import tpu as pltpu

F32 = jnp.float32
BF16 = jnp.bfloat16

NORM_EPS = 1e-6
VMEM_LIMIT_BYTES = 56 * 1024 * 1024
LANES = 128
SUBLANES = 8

ROW_TILE = 512
FFN_CHUNK = 256


def _rms(x, g):
    return x * lax.rsqrt(jnp.mean(x * x, axis=-1, keepdims=True) + NORM_EPS) * g


def _dot(a, b):
    return jnp.dot(a, b, preferred_element_type=F32)


def _resident(shape):
    zeros = (0,) * len(shape)
    return pl.BlockSpec(shape, lambda *_: zeros, pipeline_mode=pl.Buffered(1))


def _params(*semantics):
    return pltpu.CompilerParams(dimension_semantics=semantics,
                                vmem_limit_bytes=VMEM_LIMIT_BYTES)


BF16_ROWS = 16


def _rider_specs(riders):
    in_specs, out_specs, out_shapes = [], [], []
    for w, prefix, nblocks in riders:
        rows, cols = w.shape[len(prefix):]
        rb = rows // nblocks
        assert rows % nblocks == 0 and rb % BF16_ROWS == 0 and w.dtype == F32
        in_specs.append(pl.BlockSpec(
            (None,) * len(prefix) + (rb, cols),
            lambda s, prefix=prefix, nblocks=nblocks: prefix + (jnp.minimum(s, nblocks - 1), 0)))
        out_specs.append(pl.BlockSpec((rb, cols), lambda s, nblocks=nblocks: (jnp.minimum(s, nblocks - 1), 0)))
        out_shapes.append(jax.ShapeDtypeStruct((rows, cols), BF16))
    return in_specs, out_specs, out_shapes


def _with_riders(body, n_in, n_out, riders):
    nr = len(riders)

    def kernel(*refs):
        ins, rider_ins = refs[:n_in], refs[n_in:n_in + nr]
        outs = refs[n_in + nr:n_in + nr + n_out]
        rider_outs = refs[n_in + nr + n_out:n_in + 2 * nr + n_out]
        scratch = refs[n_in + 2 * nr + n_out:]
        for (_, _, nblocks), w_ref, o_ref in zip(riders, rider_ins, rider_outs):
            @pl.when(pl.program_id(0) < nblocks)
            def _(w_ref=w_ref, o_ref=o_ref):
                o_ref[...] = w_ref[...].astype(o_ref.dtype)
        body(*ins, *outs, *scratch)

    return kernel


def _mix_ffn_kernel(h_ref, ya_ref, yb_ref, wo_ref, g_ref, wg_ref, wu_ref, wd_ref,
                    ng_ref, *refs, final_norm):
    act_ref = refs[-1]
    na = ya_ref.shape[1]
    h1 = h_ref[...] + _dot(ya_ref[...], wo_ref[:na, :]) + _dot(yb_ref[...], wo_ref[na:, :])
    xn = _rms(h1, g_ref[...]).astype(BF16)
    hidden = wg_ref.shape[1]
    for j in range(hidden // FFN_CHUNK):
        cols = slice(j * FFN_CHUNK, (j + 1) * FFN_CHUNK)
        gate = _dot(xn, wg_ref[:, cols])
        up = _dot(xn, wu_ref[:, cols])
        act_ref[:, cols] = (gate * jax.nn.sigmoid(gate) * up).astype(BF16)
    out = h1 + _dot(act_ref[...], wd_ref[...])
    normed = _rms(out, ng_ref[...])
    if final_norm:
        refs[0][...] = normed
    else:
        refs[0][...] = out
        refs[1][...] = normed.astype(BF16)


def mix_ffn(h, ya, yb, wo, g, wg, wu, wd, next_g, final_norm, row_tile=ROW_TILE):
    rows, d = h.shape
    hidden = wg.shape[1]
    assert rows % row_tile == 0 and hidden % FFN_CHUNK == 0
    assert wo.shape == (ya.shape[1] + yb.shape[1], d)
    row = lambda n: pl.BlockSpec((row_tile, n), lambda i: (i, 0))
    out_shape = [jax.ShapeDtypeStruct((rows, d), F32)]
    if not final_norm:
        out_shape.append(jax.ShapeDtypeStruct((rows, d), BF16))
    outs = pl.pallas_call(
        functools.partial(_mix_ffn_kernel, final_norm=final_norm),
        grid=(rows // row_tile,),
        in_specs=[row(d), row(ya.shape[1]), row(yb.shape[1]), _resident(wo.shape), _resident((1, d)),
                  _resident(wg.shape), _resident(wu.shape), _resident(wd.shape), _resident((1, d))],
        out_specs=[row(d)] * len(out_shape),
        out_shape=out_shape,
        scratch_shapes=[pltpu.VMEM((row_tile, hidden), BF16)],
        compiler_params=_params("parallel"), name="mix_ffn",
    )(h, ya, yb, wo, g.reshape(1, d), wg, wu, wd, next_g.reshape(1, d))
    return outs[0] if final_norm else tuple(outs)


def _split3(a):
    hi = a.astype(BF16)
    r1 = a - hi.astype(F32)
    mid = r1.astype(BF16)
    lo = (r1 - mid.astype(F32)).astype(BF16)
    return hi, mid, lo


PACK = 16


def _pack3(a):
    hi, mid, lo = _split3(a)
    packed = hi.astype(F32) + pltpu.roll(mid.astype(F32), PACK, 1) + pltpu.roll(lo.astype(F32), 2 * PACK, 1)
    return packed.astype(BF16)


def _unpack3(r):
    return r + pltpu.roll(r, LANES - PACK, 1) + pltpu.roll(r, LANES - 2 * PACK, 1)


def _dot_nt(a, b):
    return lax.dot_general(a, b, (((1,), (1,)), ((), ())), preferred_element_type=F32)


def _dot_tn(a, b):
    return lax.dot_general(a, b, (((0,), (0,)), ((), ())), preferred_element_type=F32)


def _proj(hn, wt_ref, lo, hi):
    return _dot_nt(hn, wt_ref[lo:hi, :])


def _proj_narrow(hn, wt_ref, lo, hi):
    pad = jnp.zeros((LANES - (hi - lo), wt_ref.shape[1]), wt_ref.dtype)
    return _dot_nt(hn, jnp.concatenate([wt_ref[lo:hi, :], pad], axis=0))


def _silu(x):
    return x * jax.nn.sigmoid(x)


def _softplus(x):
    return jnp.maximum(x, 0.0) + jnp.log1p(jnp.exp(-jnp.abs(x)))


SSD_TILE = 256
SSD_HEADS = 16
SSD_HEADDIM = 64
SSD_GROUPS = 4
SSD_STATE = 64
SSD_CONV = 4
PROJ_PIECE = 256


def _proj_pieces(lo, hi, narrow_rows):
    pieces = [(a, min(a + PROJ_PIECE, hi), False) for a in range(lo, hi, PROJ_PIECE)]
    return pieces + [(hi, hi + narrow_rows, True)]


def _ssd_kernel(h0_ref, hnext_ref, hg_ref, wt_ref, cw_ref, cb_ref, dtb_ref, alog_ref, dsk_ref, ng_ref,
                expand_ref, tril_ref, hmask_ref, gmask_ref, bdmask_ref,
                o_ref, xpad_ref, state_ref, lhs_ref, rhs_ref, p_ref, z_ref, *, row0, nt):
    cs = hnext_ref.shape[0]
    inner = o_ref.shape[1]
    gw = inner // SSD_GROUPS
    hpg = SSD_HEADS // SSD_GROUPS
    nslab = xpad_ref.shape[0]
    conv_dim = nslab * LANES
    step = pl.program_id(0)

    def project(h_tile, pieces):
        hn = _rms(h_tile, hg_ref[...]).astype(BF16)

        def emit(count=1):
            for _ in range(min(count, len(pieces))):
                lo, hi, narrow = pieces.pop(0)
                col = lo - row0
                if narrow:
                    p_ref[:, col:col + LANES] = _proj_narrow(hn, wt_ref, lo, hi)
                else:
                    p_ref[:, col:col + hi - lo] = _proj(hn, wt_ref, lo, hi)
        return emit

    all_pieces = lambda: _proj_pieces(row0, row0 + inner + conv_dim, SSD_HEADS)

    @pl.when(step == 0)
    def _():
        project(h0_ref[...], all_pieces())(len(all_pieces()))

    @pl.when(step % nt == 0)
    def _():
        xpad_ref[:, 0:8, :] = jnp.zeros((nslab, 8, LANES), F32)
        state_ref[...] = jnp.zeros(state_ref.shape, F32)

    dt_raw = p_ref[:, inner + conv_dim:]
    z_ref[...] = p_ref[:, :inner]
    for j in range(nslab):
        xpad_ref[j, 8:8 + cs, :] = p_ref[:, inner + j * LANES:inner + (j + 1) * LANES]
    emit_next = project(hnext_ref[...], all_pieces())

    convs = []
    for j in range(nslab):
        lanes = slice(j * LANES, (j + 1) * LANES)
        conv = cb_ref[:, lanes]
        for k in range(SSD_CONV):
            lo = 8 - (SSD_CONV - 1) + k
            conv = conv + cw_ref[k:k + 1, lanes] * xpad_ref[j, lo:lo + cs, :]
        xpad_ref[j, 0:8, :] = xpad_ref[j, cs:cs + 8, :]
        convs.append(conv)
        if j % 3 == 2:
            emit_next()
    xc = _silu(jnp.concatenate(convs, axis=1))
    xs = xc[:, :inner]
    bm = xc[:, inner:inner + SSD_GROUPS * SSD_STATE]
    cm = xc[:, inner + SSD_GROUPS * SSD_STATE:]

    head_lane = lax.broadcasted_iota(jnp.int32, (cs, LANES), 1) < SSD_HEADS
    dt = jnp.where(head_lane, _softplus(dt_raw + dtb_ref[...]), 0.0)
    da = dt * -jnp.exp(alog_ref[...])
    a_cum = jnp.where(head_lane, _unpack3(_dot(tril_ref[...], _pack3(da))), 0.0)
    a_cum_t = a_cum.T
    expand = expand_ref[...]
    xdt = xs * _dot(_pack3(dt), expand)
    a_x = _dot(_pack3(a_cum), expand)
    a_last = a_x[cs - 1:cs, :]
    decay_in = jnp.exp(a_x)
    xd = (xdt * jnp.exp(a_last - a_x)).astype(BF16)
    xdt = xdt.astype(BF16)

    state = state_ref[...]
    y = _dot(cm.astype(BF16), state.astype(BF16)) * decay_in
    new_state = state * jnp.exp(a_last) + _dot(bm.T.astype(BF16), xd)
    state_ref[...] = new_state * bdmask_ref[...]

    row = lax.broadcasted_iota(jnp.int32, (cs, cs), 0)
    col = lax.broadcasted_iota(jnp.int32, (cs, cs), 1)
    causal = row >= col
    bm16 = bm.astype(BF16)
    ys = []
    for g in range(SSD_GROUPS):
        cb = _dot_nt((cm * gmask_ref[g:g + 1, :]).astype(BF16), bm16)
        xg = xdt[:, g * gw:(g + 1) * gw]
        for j in range(hpg):
            h = g * hpg + j
            diff = a_cum[:, h:h + 1] - a_cum_t[h:h + 1, :]
            seg = jnp.exp(jnp.where(causal, diff, -jnp.inf))
            lhs_ref[:, j * cs:(j + 1) * cs] = (cb * seg).astype(BF16)
            rhs_ref[j * cs:(j + 1) * cs, :] = xg * hmask_ref[j:j + 1, :]
        ys.append(_dot(lhs_ref[...], rhs_ref[...]))
        emit_next(2)
    y = y + jnp.concatenate(ys, axis=1) + dsk_ref[...] * xs
    y = y * _silu(z_ref[...])
    o_ref[...] = _rms(y, ng_ref[...]).astype(o_ref.dtype)
    emit_next(len(all_pieces()))


def ssd_mixer(h, h_norm_g, wt_in, row0, conv_w, conv_b, dt_bias, a_log, d_skip, norm_g, batch, seqlen, riders=()):
    cs = SSD_TILE
    d_model = h.shape[1]
    inner = SSD_HEADS * SSD_HEADDIM
    conv_dim = conv_w.shape[1]
    gw = inner // SSD_GROUPS
    gs = SSD_GROUPS * SSD_STATE
    hpg = SSD_HEADS // SSD_GROUPS
    nt = seqlen // cs
    assert seqlen % cs == 0 and h.shape[0] == batch * seqlen and conv_dim % LANES == 0
    assert wt_in.shape[1] == d_model and row0 + inner + conv_dim + SSD_HEADS <= wt_in.shape[0]

    pad = lambda v: jnp.pad(v.astype(F32), (0, LANES - SSD_HEADS)).reshape(1, LANES)
    assert SSD_HEADS == PACK
    lane_head = np.arange(inner) // SSD_HEADDIM
    packed_head = np.where(np.arange(LANES) < 3 * PACK, np.arange(LANES) % PACK, -1)
    expand = jnp.asarray(packed_head[:, None] == lane_head[None, :], BF16)
    tril = jnp.asarray(np.arange(cs)[:, None] >= np.arange(cs)[None, :], BF16)
    hmask = jnp.asarray(np.arange(hpg)[:, None] == (np.arange(gw) // SSD_HEADDIM)[None, :], BF16)
    gmask = jnp.asarray(np.arange(SSD_GROUPS)[:, None] == (np.arange(gs) // SSD_STATE)[None, :], F32)
    bdmask = jnp.asarray((np.arange(gs) // SSD_STATE)[:, None] == (np.arange(inner) // gw)[None, :], F32)

    ntiles = batch * nt
    consts = [h_norm_g.astype(F32).reshape(1, d_model), wt_in,
              conv_w.astype(F32), conv_b.astype(F32).reshape(1, -1), pad(dt_bias), pad(a_log),
              jnp.repeat(d_skip.astype(F32), SSD_HEADDIM).reshape(1, inner), norm_g.astype(F32).reshape(1, inner),
              expand, tril, hmask, gmask, bdmask]
    rider_in, rider_out, rider_shapes = _rider_specs(riders)
    body = functools.partial(_ssd_kernel, row0=row0, nt=nt)
    return pl.pallas_call(
        _with_riders(body, 2 + len(consts), 1, riders), grid=(ntiles,),
        in_specs=[pl.BlockSpec((cs, d_model), lambda s: (0, 0)),
                  pl.BlockSpec((cs, d_model), lambda s: (jnp.minimum(s + 1, ntiles - 1), 0))]
        + [_resident(c.shape) for c in consts] + rider_in,
        out_specs=[pl.BlockSpec((cs, inner), lambda s: (s, 0))] + rider_out,
        out_shape=[jax.ShapeDtypeStruct((batch * seqlen, inner), BF16)] + rider_shapes,
        scratch_shapes=[pltpu.VMEM((conv_dim // LANES, cs + 8, LANES), F32), pltpu.VMEM((gs, inner), F32),
                        pltpu.VMEM((cs, hpg * cs), BF16), pltpu.VMEM((hpg * cs, gw), BF16),
                        pltpu.VMEM((cs, inner + conv_dim + LANES), F32), pltpu.VMEM((cs, inner), F32)],
        compiler_params=_params("arbitrary"), name="ssd_mixer",
    )(h, h, *consts, *[w for w, _, _ in riders])


S5_TILE = 128
S5_CHUNK = 32
S5_GROUP = 16
S5_STATE = 64
S5_GROUPS_PER_BLOCK = 16


def _s5_kernel(h_ref, ng_ref, wt_ref, wb_ref, a_ref, wc_ref, d_ref, gw_ref, gb_ref, o_ref,
               bu_ref, x_ref, y_ref, abc_ref):
    bsz, tl, d_model = h_ref.shape
    width = d_ref.shape[1]
    nblk = wb_ref.shape[0]
    kw = wb_ref.shape[1]
    sw = a_ref.shape[2]
    rows = tl * bsz

    lanes = lambda kb: slice(kb * 2 * sw, (kb + 1) * 2 * sw)
    re = lambda kb: slice(kb * 2 * sw, kb * 2 * sw + sw)
    im = lambda kb: slice(kb * 2 * sw + sw, (kb + 1) * 2 * sw)

    @pl.when(pl.program_id(0) == 0)
    def _():
        x_ref[...] = jnp.zeros(x_ref.shape, F32)
        for kb in range(nblk):
            abc_ref[:, re(kb)] = jnp.broadcast_to(a_ref[kb, 0:1, :], (bsz, sw))
            abc_ref[:, im(kb)] = jnp.broadcast_to(a_ref[kb, 1:2, :], (bsz, sw))

    hn = _rms(h_ref[...].reshape(rows, d_model), ng_ref[...]).astype(BF16)
    u = _proj(hn, wt_ref, 0, width)
    u = jnp.swapaxes(u.reshape(bsz, tl, width), 0, 1).reshape(rows, width)
    u16 = u.astype(BF16)

    crows = S5_CHUNK * bsz
    nchunk = tl // S5_CHUNK

    def input_stage(j):
        rows_j = slice(j * crows, (j + 1) * crows)
        for kb in range(nblk):
            bu_ref[rows_j, lanes(kb)] = _dot(u16[rows_j, kb * kw:(kb + 1) * kw], wb_ref[kb])

    def scan_stage(j, state):
        for t in range(j * S5_CHUNK, (j + 1) * S5_CHUNK):
            r = slice(t * bsz, (t + 1) * bsz)
            new_state = []
            for kb in range(nblk):
                x_re, x_im = state[kb]
                a_re, a_im = abc_ref[:, re(kb)], abc_ref[:, im(kb)]
                n_re = a_re * x_re - a_im * x_im + bu_ref[r, re(kb)]
                n_im = a_re * x_im + a_im * x_re + bu_ref[r, im(kb)]
                bu_ref[r, re(kb)] = n_re
                bu_ref[r, im(kb)] = n_im
                new_state.append((n_re, n_im))
            state = new_state
        return state

    def output_stage(j):
        rows_j = slice(j * crows, (j + 1) * crows)
        ys = [_dot(bu_ref[rows_j, lanes(kb)].astype(BF16), wc_ref[kb]) for kb in range(nblk)]
        y = jnp.concatenate(ys, axis=1) + d_ref[...] * u[rows_j, :]
        z = jax.nn.gelu(y)
        y_ref[rows_j, :] = z * jax.nn.sigmoid(_dot(z.astype(BF16), gw_ref[...]) + gb_ref[...])

    state = [(x_ref[:, re(kb)], x_ref[:, im(kb)]) for kb in range(nblk)]
    input_stage(0)
    for j in range(nchunk):
        if j + 1 < nchunk:
            input_stage(j + 1)
        state = scan_stage(j, state)
        output_stage(j)
    for kb in range(nblk):
        x_ref[:, re(kb)] = state[kb][0]
        x_ref[:, im(kb)] = state[kb][1]
    o_ref[...] = jnp.swapaxes(y_ref[...].reshape(tl, bsz, width), 0, 1).astype(o_ref.dtype)


def _s5_weights(lam_re, lam_im, log_step, b_re, b_im, c_re, c_im):
    lr, li = lam_re.astype(F32), lam_im.astype(F32)
    step = jnp.exp(log_step.astype(F32))[:, None]
    mag = jnp.exp(lr * step)
    a_re, a_im = mag * jnp.cos(li * step), mag * jnp.sin(li * step)
    den = lr * lr + li * li
    k_re = ((a_re - 1.0) * lr + a_im * li) / den
    k_im = (a_im * lr - (a_re - 1.0) * li) / den
    br, bi = b_re.astype(F32), b_im.astype(F32)
    bb_re = k_re[..., None] * br - k_im[..., None] * bi
    bb_im = k_re[..., None] * bi + k_im[..., None] * br
    groups, nstate, nch = br.shape
    gpb = S5_GROUPS_PER_BLOCK
    nblk = groups // gpb
    eye = jnp.eye(gpb, dtype=F32)

    def in_block(bb):
        t = bb.reshape(nblk, gpb, nstate, nch).transpose(0, 1, 3, 2)
        t = t[:, :, :, None, :] * eye[None, :, None, :, None]
        return t.reshape(nblk, gpb * nch, gpb * nstate)

    def out_block(c):
        t = c.astype(F32).reshape(nblk, gpb, nch, nstate).transpose(0, 1, 3, 2)
        t = t[:, :, :, None, :] * eye[None, :, None, :, None]
        return t.reshape(nblk, gpb * nstate, gpb * nch)

    wb = jnp.concatenate([in_block(bb_re), in_block(bb_im)], axis=2).astype(BF16)
    wc = jnp.concatenate([out_block(c_re), -out_block(c_im)], axis=1).astype(BF16)
    a = jnp.stack([a_re.reshape(nblk, gpb * nstate), a_im.reshape(nblk, gpb * nstate)], axis=1)
    return wb, a, wc


def s5_mixer(h, norm_g, wt_in, lam_re, lam_im, log_step, b_re, b_im, c_re, c_im, d_skip, glu_w, glu_b, riders=()):
    bsz, seqlen, d_model = h.shape
    width = glu_w.shape[0]
    tl = S5_TILE
    assert bsz == SUBLANES and seqlen % tl == 0
    wb, a, wc = _s5_weights(lam_re, lam_im, log_step, b_re, b_im, c_re, c_im)
    nstates = wb.shape[0] * wb.shape[2]
    consts = [norm_g.astype(F32).reshape(1, d_model), wt_in, wb, a, wc, d_skip.astype(F32).reshape(1, width),
              glu_w.astype(BF16), glu_b.astype(F32).reshape(1, width)]
    tok = lambda n: pl.BlockSpec((bsz, tl, n), lambda i: (0, i, 0))
    const_specs = [_resident(c.shape) for c in consts]
    const_specs[1] = pl.BlockSpec((width, d_model), lambda i: (0, 0), pipeline_mode=pl.Buffered(1))
    rider_in, rider_out, rider_shapes = _rider_specs(riders)
    return pl.pallas_call(
        _with_riders(_s5_kernel, 1 + len(consts), 1, riders), grid=(seqlen // tl,),
        in_specs=[tok(d_model)] + const_specs + rider_in,
        out_specs=[tok(width)] + rider_out,
        out_shape=[jax.ShapeDtypeStruct((bsz, seqlen, width), BF16)] + rider_shapes,
        scratch_shapes=[pltpu.VMEM((tl * bsz, nstates), F32), pltpu.VMEM((bsz, nstates), F32),
                        pltpu.VMEM((tl * bsz, width), F32), pltpu.VMEM((bsz, nstates), F32)],
        compiler_params=_params("arbitrary"), name="s5_mixer",
    )(h, *consts, *[w for w, _, _ in riders])


MIX_TILE = 256
MIX_HEADS = 4
MIX_QK = 128
MIX_V = 256
ROPE_BASE = 10000.0


def _rotary(t, cos, sin_signed):
    return t * cos + pltpu.roll(t, MIX_QK // 2, axis=1) * sin_signed


def _project_head(hn, wt_ref, p_ref, row0, g_row0, h):
    kw, vw = MIX_HEADS * MIX_QK, MIX_HEADS * MIX_V
    q_lo, k_lo, v_lo = h * MIX_QK, kw + h * MIX_QK, 2 * kw + h * MIX_V
    w_qk = jnp.concatenate([wt_ref[row0 + q_lo:row0 + q_lo + MIX_QK, :],
                            wt_ref[row0 + k_lo:row0 + k_lo + MIX_QK, :]], axis=0)
    qk = _dot_nt(hn, w_qk)
    p_ref[:, q_lo:q_lo + MIX_QK] = qk[:, :MIX_QK]
    p_ref[:, k_lo:k_lo + MIX_QK] = qk[:, MIX_QK:]
    p_ref[:, v_lo:v_lo + MIX_V] = _proj(hn, wt_ref, row0 + v_lo, row0 + v_lo + MIX_V)
    g_lo = h * MIX_V
    p_ref[:, 2 * kw + vw + g_lo:2 * kw + vw + g_lo + MIX_V] = _proj(hn, wt_ref, g_row0 + g_lo, g_row0 + g_lo + MIX_V)


def _ret_kernel(hn0_ref, hnext_ref, wt_ref, cos_ref, sin_ref, dmat_ref, qdec_ref, kdec_ref, cdec_ref,
                ng_ref, o_ref, state_ref, p_ref, *, row0, nt):
    step = pl.program_id(0)
    kw, vw = MIX_HEADS * MIX_QK, MIX_HEADS * MIX_V
    g_row0 = row0 + 2 * kw + vw

    @pl.when(step == 0)
    def _():
        for h in range(MIX_HEADS):
            _project_head(hn0_ref[...], wt_ref, p_ref, row0, g_row0, h)

    @pl.when(step % nt == 0)
    def _():
        state_ref[...] = jnp.zeros(state_ref.shape, F32)

    hn_next = hnext_ref[...]
    cos, sin = cos_ref[...], sin_ref[...]
    scale = MIX_QK ** -0.5
    for h in range(MIX_HEADS):
        vl = slice(h * MIX_V, (h + 1) * MIX_V)
        q = _rotary(p_ref[:, h * MIX_QK:(h + 1) * MIX_QK], cos, sin)
        k = _rotary(p_ref[:, kw + h * MIX_QK:kw + (h + 1) * MIX_QK], cos, sin) * scale
        v16 = p_ref[:, 2 * kw + h * MIX_V:2 * kw + (h + 1) * MIX_V].astype(BF16)
        gate = _silu(p_ref[:, 2 * kw + vw + h * MIX_V:2 * kw + vw + (h + 1) * MIX_V])
        q16 = q.astype(BF16)
        inner = _dot_nt(q16, k.astype(BF16)) * dmat_ref[h]
        state = state_ref[h]
        out = _dot(inner.astype(BF16), v16) + _dot_nt((q * qdec_ref[h]).astype(BF16), state.astype(BF16))
        state_ref[h] = state * cdec_ref[h] + _dot_tn(v16, (k * kdec_ref[h]).astype(BF16))
        _project_head(hn_next, wt_ref, p_ref, row0, g_row0, h)
        mu = jnp.mean(out, axis=-1, keepdims=True)
        cen = out - mu
        var = jnp.mean(cen * cen, axis=-1, keepdims=True)
        o = cen * lax.rsqrt(var + NORM_EPS) * ng_ref[:, vl]
        o_ref[:, vl] = (o * gate).astype(o_ref.dtype)


def _rope_tables(seqlen):
    half = MIX_QK // 2
    inv_freq = (np.float32(ROPE_BASE) ** (-np.arange(half, dtype=np.float32) / half)).astype(np.float32)
    ang = (np.arange(seqlen, dtype=np.float32)[:, None] * inv_freq[None, :]).astype(np.float64)
    cos, sin = np.cos(ang), np.sin(ang)
    return (jnp.asarray(np.concatenate([cos, cos], axis=1), F32),
            jnp.asarray(np.concatenate([-sin, sin], axis=1), F32))


def retention_mixer(hn, wt_in, row0, norm_g, batch, seqlen):
    c = MIX_TILE
    nt = seqlen // c
    d_model = hn.shape[1]
    kw, vw = MIX_HEADS * MIX_QK, MIX_HEADS * MIX_V
    assert seqlen % c == 0 and hn.shape[0] == batch * seqlen
    assert wt_in.shape[1] == d_model and row0 + 2 * kw + 2 * vw <= wt_in.shape[0]
    cos, sin = _rope_tables(seqlen)
    log_gamma = np.log(1.0 - 2.0 ** (-5.0 - np.arange(MIX_HEADS, dtype=np.float64)))
    pos = np.arange(c, dtype=np.float64)
    diff = pos[:, None] - pos[None, :]
    dmat = jnp.asarray(np.where(diff >= 0, np.exp(log_gamma[:, None, None] * np.maximum(diff, 0.0)), 0.0), F32)
    lanes = lambda col: jnp.asarray(np.broadcast_to(col[:, :, None], (MIX_HEADS, c, MIX_QK)), F32)
    qdec = lanes(np.exp(log_gamma[:, None] * (pos + 1.0)))
    kdec = lanes(np.exp(log_gamma[:, None] * (c - 1.0 - pos)))
    cdec = jnp.asarray(np.broadcast_to(np.exp(log_gamma * c)[:, None, None], (MIX_HEADS, 1, MIX_QK)), F32)
    consts = [dmat, qdec, kdec, cdec, norm_g.astype(F32).reshape(1, vw)]
    ntiles = batch * nt
    rope = pl.BlockSpec((c, MIX_QK), lambda s: (s % nt, 0))
    return pl.pallas_call(
        functools.partial(_ret_kernel, row0=row0, nt=nt), grid=(ntiles,),
        in_specs=[pl.BlockSpec((c, d_model), lambda s: (0, 0)),
                  pl.BlockSpec((c, d_model), lambda s: (jnp.minimum(s + 1, ntiles - 1), 0)),
                  _resident(wt_in.shape), rope, rope] + [_resident(x.shape) for x in consts],
        out_specs=pl.BlockSpec((c, vw), lambda s: (s, 0)),
        out_shape=jax.ShapeDtypeStruct((batch * seqlen, vw), BF16),
        scratch_shapes=[pltpu.VMEM((MIX_HEADS, MIX_V, MIX_QK), F32), pltpu.VMEM((c, 2 * kw + 2 * vw), F32)],
        compiler_params=_params("arbitrary"), name="retention_mixer",
    )(hn, hn, wt_in, cos, sin, *consts)


GLA_CHUNK = 64
GLA_TAU = 16.0


def _log_sigmoid(x):
    return jnp.minimum(x, 0.0) - jnp.log1p(jnp.exp(-jnp.abs(x)))


def _gla_kernel(hn0_ref, hnext_ref, wt_ref, gw_ref, gb_ref, ng_ref, cumsel_ref,
                o_ref, state_ref, p_ref, *, row0, rank, nt):
    tl = hnext_ref.shape[0]
    nchunk = tl // GLA_CHUNK
    kw, vw = MIX_HEADS * MIX_QK, MIX_HEADS * MIX_V
    step = pl.program_id(0)
    a_row0 = row0 + 2 * kw + vw
    g_row0 = a_row0 + rank
    a_col0 = 2 * kw + 2 * vw

    @pl.when(step == 0)
    def _():
        hn0 = hn0_ref[...]
        p_ref[:, a_col0:] = _proj_narrow(hn0, wt_ref, a_row0, a_row0 + rank)
        for h in range(MIX_HEADS):
            _project_head(hn0, wt_ref, p_ref, row0, g_row0, h)

    @pl.when(step % nt == 0)
    def _():
        state_ref[...] = jnp.zeros(state_ref.shape, F32)

    hn_next = hnext_ref[...]
    a_lr = p_ref[:, a_col0:]
    log_a = _log_sigmoid(_dot(a_lr.astype(BF16), gw_ref[...]) + gb_ref[...]) / GLA_TAU
    p_ref[:, a_col0:] = _proj_narrow(hn_next, wt_ref, a_row0, a_row0 + rank)
    hi, mid, lo = _split3(log_a)
    zeros = jnp.zeros((GLA_CHUNK, kw), BF16)
    bcums, blasts = [], []
    for c in range(nchunk):
        rows = slice(c * GLA_CHUNK, (c + 1) * GLA_CHUNK)
        terms = jnp.concatenate([hi[rows, :], mid[rows, :], lo[rows, :], zeros], axis=0)
        bc = _dot(cumsel_ref[...], terms)
        bcums.append(bc)
        blasts.append(jnp.broadcast_to(bc[GLA_CHUNK - 1:GLA_CHUNK, :], (GLA_CHUNK, kw)))
    bcum = jnp.concatenate(bcums, axis=0)
    blast = jnp.concatenate(blasts, axis=0)
    grow = jnp.exp(bcum)
    row = lax.broadcasted_iota(jnp.int32, (tl, tl), 0)
    col = lax.broadcasted_iota(jnp.int32, (tl, tl), 1)
    shift = GLA_CHUNK.bit_length() - 1
    keep = (row >= col) & (jnp.right_shift(row, shift) == jnp.right_shift(col, shift))
    bcum_t = bcum.T
    scale = MIX_QK ** -0.5
    for h in range(MIX_HEADS):
        kl = slice(h * MIX_QK, (h + 1) * MIX_QK)
        vl = slice(h * MIX_V, (h + 1) * MIX_V)
        k = p_ref[:, kw + h * MIX_QK:kw + (h + 1) * MIX_QK]
        q_in = (p_ref[:, kl] * scale * grow[:, kl]).astype(BF16)
        k_in = (k * jnp.exp(-bcum[:, kl])).astype(BF16)
        k_out = (k * jnp.exp(blast[:, kl] - bcum[:, kl])).astype(BF16)
        v = p_ref[:, 2 * kw + h * MIX_V:2 * kw + (h + 1) * MIX_V]
        gate = _silu(p_ref[:, 2 * kw + vw + h * MIX_V:2 * kw + vw + (h + 1) * MIX_V])
        v16 = v.astype(BF16)
        att = jnp.where(keep, _dot_nt(q_in, k_in), 0.0)
        out = _dot(att.astype(BF16), v16)
        _project_head(hn_next, wt_ref, p_ref, row0, g_row0, h)
        chunks = [slice(c * GLA_CHUNK, (c + 1) * GLA_CHUNK) for c in range(nchunk)]
        kvs = [_dot_tn(k_out[rows, :], v16[rows, :]) for rows in chunks]
        decays = [jnp.exp(bcum_t[h * MIX_QK:(h + 1) * MIX_QK, rows.stop - 1:rows.stop]) for rows in chunks]
        state = state_ref[h]
        inter = []
        for rows, kv, decay in zip(chunks, kvs, decays):
            inter.append(_dot(q_in[rows, :], state.astype(BF16)))
            state = state * decay + kv
        state_ref[h] = state
        out = out + jnp.concatenate(inter, axis=0)
        o = out * lax.rsqrt(jnp.mean(out * out, axis=-1, keepdims=True) + NORM_EPS) * ng_ref[:, vl]
        o_ref[:, vl] = (o * gate).astype(o_ref.dtype)


def gla_mixer(hn, wt_in, row0, gate_w, gate_b, norm_g, batch, seqlen):
    tl = MIX_TILE
    nt = seqlen // tl
    d_model = hn.shape[1]
    kw, vw = MIX_HEADS * MIX_QK, MIX_HEADS * MIX_V
    assert seqlen % tl == 0 and tl % GLA_CHUNK == 0 and hn.shape[0] == batch * seqlen
    rank = gate_w.shape[0]
    assert wt_in.shape[1] == d_model and row0 + 2 * kw + 2 * vw + rank <= wt_in.shape[0]
    gw = jnp.pad(gate_w.astype(F32), ((0, LANES - rank), (0, 0))).astype(BF16)
    t = np.arange(GLA_CHUNK)
    tril = t[:, None] >= t[None, :]
    cumsel = jnp.asarray(np.concatenate([tril, tril, tril, np.zeros_like(tril)], axis=1), BF16)
    consts = [gw, gate_b.astype(F32).reshape(1, kw), norm_g.astype(F32).reshape(1, vw), cumsel]
    ntiles = batch * nt
    return pl.pallas_call(
        functools.partial(_gla_kernel, row0=row0, rank=rank, nt=nt), grid=(ntiles,),
        in_specs=[pl.BlockSpec((tl, d_model), lambda s: (0, 0)),
                  pl.BlockSpec((tl, d_model), lambda s: (jnp.minimum(s + 1, ntiles - 1), 0)),
                  _resident(wt_in.shape)] + [_resident(x.shape) for x in consts],
        out_specs=pl.BlockSpec((tl, vw), lambda s: (s, 0)),
        out_shape=jax.ShapeDtypeStruct((batch * seqlen, vw), BF16),
        scratch_shapes=[pltpu.VMEM((MIX_HEADS, MIX_QK, MIX_V), F32),
                        pltpu.VMEM((tl, 2 * kw + 2 * vw + LANES), F32)],
        compiler_params=_params("arbitrary"), name="gla_mixer",
    )(hn, hn, wt_in, *consts)


def _row_blocks(rows, max_blocks):
    return max(n for n in range(1, max_blocks + 1) if rows % n == 0 and (rows // n) % BF16_ROWS == 0)


def kernel(x, norm_mix_g, norm_ffn_g, final_norm_g, ev_in_w, s5_lam_re, s5_lam_im, s5_log_step, s5_b_re, s5_b_im, s5_c_re, s5_c_im, s5_d, s5_glu_w, s5_glu_b, ssd_conv_w, ssd_conv_b, ssd_dt_bias, ssd_a_log, ssd_d, ssd_norm_g, ev_out_w, od_in_w, ret_norm_g, gla_gate_w, gla_gate_b, gla_norm_g, od_out_w, ffn_gate_w, ffn_up_w, ffn_down_w):
    bsz, seqlen, d = x.shape
    rows = bsz * seqlen
    depth = norm_mix_g.shape[0]
    assert depth == 2 and ev_in_w.shape[0] == 1 and od_in_w.shape[0] == 1, "even layer then odd layer"
    h = x.reshape(rows, d)
    kw, vw = gla_gate_w.shape[2], ret_norm_g.shape[1]
    s5_w = s5_glu_w.shape[1]
    hidden = ffn_gate_w.shape[2]

    wt_even = jnp.swapaxes(ev_in_w[0], 0, 1).astype(BF16)
    s5_steps, ssd_steps = seqlen // S5_TILE, rows // SSD_TILE
    y_a, wg0, wu0, wd0, wo0 = s5_mixer(
        h.reshape(bsz, seqlen, d), norm_mix_g[0], wt_even, s5_lam_re[0], s5_lam_im[0], s5_log_step[0],
        s5_b_re[0], s5_b_im[0], s5_c_re[0], s5_c_im[0], s5_d[0].reshape(-1), s5_glu_w[0], s5_glu_b[0],
        riders=((ffn_gate_w, (0,), s5_steps), (ffn_up_w, (0,), s5_steps), (ffn_down_w, (0,), s5_steps),
                (ev_out_w, (0,), s5_steps)))
    wt_odd_f32 = jnp.swapaxes(od_in_w[0], 0, 1)
    y_b, wg1, wu1, wd1, wo1, wt_odd = ssd_mixer(
        h, norm_mix_g[0], wt_even, s5_w, ssd_conv_w[0], ssd_conv_b[0], ssd_dt_bias[0], ssd_a_log[0],
        ssd_d[0], ssd_norm_g[0], bsz, seqlen,
        riders=((ffn_gate_w, (1,), ssd_steps), (ffn_up_w, (1,), ssd_steps),
                (ffn_down_w, (1,), _row_blocks(hidden, ssd_steps)), (od_out_w, (0,), ssd_steps),
                (wt_odd_f32, (), _row_blocks(wt_odd_f32.shape[0], ssd_steps))))
    h, hn = mix_ffn(h, y_a.reshape(rows, s5_w), y_b, wo0, norm_ffn_g[0], wg0, wu0, wd0, norm_mix_g[1],
                    final_norm=False)

    y_a = retention_mixer(hn, wt_odd, 0, ret_norm_g[0], bsz, seqlen)
    y_b = gla_mixer(hn, wt_odd, 2 * kw + 2 * vw, gla_gate_w[0], gla_gate_b[0], gla_norm_g[0], bsz, seqlen)
    h = mix_ffn(h, y_a, y_b, wo1, norm_ffn_g[1], wg1, wu1, wd1, final_norm_g, final_norm=True)
    return h.reshape(bsz, seqlen, d)
```

```python
import functools

import jax
import jax.numpy as jnp
import numpy as np
from jax import lax
from jax.experimental import pallas as pl
from jax.experimental.pallas import tpu as pltpu

F32 = jnp.float32
BF16 = jnp.bfloat16

NORM_EPS = 1e-6
VMEM_LIMIT_BYTES = 56 * 1024 * 1024
LANES = 128
SUBLANES = 8

ROW_TILE = 512
FFN_CHUNK = 256


def _rms(x, g):
    return x * lax.rsqrt(jnp.mean(x * x, axis=-1, keepdims=True) + NORM_EPS) * g


def _dot(a, b):
    return jnp.dot(a, b, preferred_element_type=F32)


def _resident(shape):
    zeros = (0,) * len(shape)
    return pl.BlockSpec(shape, lambda *_: zeros, pipeline_mode=pl.Buffered(1))


def _params(*semantics):
    return pltpu.CompilerParams(dimension_semantics=semantics,
                                vmem_limit_bytes=VMEM_LIMIT_BYTES)


BF16_ROWS = 16


def _rider_specs(riders):
    in_specs, out_specs, out_shapes = [], [], []
    for w, prefix, nblocks in riders:
        rows, cols = w.shape[len(prefix):]
        rb = rows // nblocks
        assert rows % nblocks == 0 and rb % BF16_ROWS == 0 and w.dtype == F32
        in_specs.append(pl.BlockSpec(
            (None,) * len(prefix) + (rb, cols),
            lambda s, prefix=prefix, nblocks=nblocks: prefix + (jnp.minimum(s, nblocks - 1), 0)))
        out_specs.append(pl.BlockSpec((rb, cols), lambda s, nblocks=nblocks: (jnp.minimum(s, nblocks - 1), 0)))
        out_shapes.append(jax.ShapeDtypeStruct((rows, cols), BF16))
    return in_specs, out_specs, out_shapes


def _with_riders(body, n_in, n_out, riders):
    nr = len(riders)

    def kernel(*refs):
        ins, rider_ins = refs[:n_in], refs[n_in:n_in + nr]
        outs = refs[n_in + nr:n_in + nr + n_out]
        rider_outs = refs[n_in + nr + n_out:n_in + 2 * nr + n_out]
        scratch = refs[n_in + 2 * nr + n_out:]
        for (_, _, nblocks), w_ref, o_ref in zip(riders, rider_ins, rider_outs):
            @pl.when(pl.program_id(0) < nblocks)
            def _(w_ref=w_ref, o_ref=o_ref):
                o_ref[...] = w_ref[...].astype(o_ref.dtype)
        body(*ins, *outs, *scratch)

    return kernel


def _mix_ffn_kernel(h_ref, ya_ref, yb_ref, wo_ref, g_ref, wg_ref, wu_ref, wd_ref,
                    ng_ref, *refs, final_norm):
    act_ref = refs[-1]
    na = ya_ref.shape[1]
    h1 = h_ref[...] + _dot(ya_ref[...], wo_ref[:na, :]) + _dot(yb_ref[...], wo_ref[na:, :])
    xn = _rms(h1, g_ref[...]).astype(BF16)
    hidden = wg_ref.shape[1]
    for j in range(hidden // FFN_CHUNK):
        cols = slice(j * FFN_CHUNK, (j + 1) * FFN_CHUNK)
        gate = _dot(xn, wg_ref[:, cols])
        up = _dot(xn, wu_ref[:, cols])
        act_ref[:, cols] = (gate * jax.nn.sigmoid(gate) * up).astype(BF16)
    out = h1 + _dot(act_ref[...], wd_ref[...])
    normed = _rms(out, ng_ref[...])
    if final_norm:
        refs[0][...] = normed
    else:
        refs[0][...] = out
        refs[1][...] = normed.astype(BF16)


def mix_ffn(h, ya, yb, wo, g, wg, wu, wd, next_g, final_norm, row_tile=ROW_TILE):
    rows, d = h.shape
    hidden = wg.shape[1]
    assert rows % row_tile == 0 and hidden % FFN_CHUNK == 0
    assert wo.shape == (ya.shape[1] + yb.shape[1], d)
    row = lambda n: pl.BlockSpec((row_tile, n), lambda i: (i, 0))
    out_shape = [jax.ShapeDtypeStruct((rows, d), F32)]
    if not final_norm:
        out_shape.append(jax.ShapeDtypeStruct((rows, d), BF16))
    outs = pl.pallas_call(
        functools.partial(_mix_ffn_kernel, final_norm=final_norm),
        grid=(rows // row_tile,),
        in_specs=[row(d), row(ya.shape[1]), row(yb.shape[1]), _resident(wo.shape), _resident((1, d)),
                  _resident(wg.shape), _resident(wu.shape), _resident(wd.shape), _resident((1, d))],
        out_specs=[row(d)] * len(out_shape),
        out_shape=out_shape,
        scratch_shapes=[pltpu.VMEM((row_tile, hidden), BF16)],
        compiler_params=_params("parallel"), name="mix_ffn",
    )(h, ya, yb, wo, g.reshape(1, d), wg, wu, wd, next_g.reshape(1, d))
    return outs[0] if final_norm else tuple(outs)


def _split3(a):
    hi = a.astype(BF16)
    r1 = a - hi.astype(F32)
    mid = r1.astype(BF16)
    lo = (r1 - mid.astype(F32)).astype(BF16)
    return hi, mid, lo


PACK = 16


def _pack3(a):
    hi, mid, lo = _split3(a)
    packed = hi.astype(F32) + pltpu.roll(mid.astype(F32), PACK, 1) + pltpu.roll(lo.astype(F32), 2 * PACK, 1)
    return packed.astype(BF16)


def _unpack3(r):
    return r + pltpu.roll(r, LANES - PACK, 1) + pltpu.roll(r, LANES - 2 * PACK, 1)


def _dot_nt(a, b):
    return lax.dot_general(a, b, (((1,), (1,)), ((), ())), preferred_element_type=F32)


def _dot_tn(a, b):
    return lax.dot_general(a, b, (((0,), (0,)), ((), ())), preferred_element_type=F32)


def _proj(hn, wt_ref, lo, hi):
    return _dot_nt(hn, wt_ref[lo:hi, :])


def _proj_narrow(hn, wt_ref, lo, hi):
    pad = jnp.zeros((LANES - (hi - lo), wt_ref.shape[1]), wt_ref.dtype)
    return _dot_nt(hn, jnp.concatenate([wt_ref[lo:hi, :], pad], axis=0))


def _silu(x):
    return x * jax.nn.sigmoid(x)


def _softplus(x):
    return jnp.maximum(x, 0.0) + jnp.log1p(jnp.exp(-jnp.abs(x)))


SSD_TILE = 256
SSD_HEADS = 16
SSD_HEADDIM = 64
SSD_GROUPS = 4
SSD_STATE = 64
SSD_CONV = 4
PROJ_PIECE = 256


def _proj_pieces(lo, hi, narrow_rows):
    pieces = [(a, min(a + PROJ_PIECE, hi), False) for a in range(lo, hi, PROJ_PIECE)]
    return pieces + [(hi, hi + narrow_rows, True)]


def _ssd_kernel(h0_ref, hnext_ref, hg_ref, wt_ref, cw_ref, cb_ref, dtb_ref, alog_ref, dsk_ref, ng_ref,
                expand_ref, tril_ref, hmask_ref, gmask_ref, bdmask_ref,
                o_ref, xpad_ref, state_ref, lhs_ref, rhs_ref, p_ref, z_ref, *, row0, nt):
    cs = hnext_ref.shape[0]
    inner = o_ref.shape[1]
    gw = inner // SSD_GROUPS
    hpg = SSD_HEADS // SSD_GROUPS
    nslab = xpad_ref.shape[0]
    conv_dim = nslab * LANES
    step = pl.program_id(0)

    def project(h_tile, pieces):
        hn = _rms(h_tile, hg_ref[...]).astype(BF16)

        def emit(count=1):
            for _ in range(min(count, len(pieces))):
                lo, hi, narrow = pieces.pop(0)
                col = lo - row0
                if narrow:
                    p_ref[:, col:col + LANES] = _proj_narrow(hn, wt_ref, lo, hi)
                else:
                    p_ref[:, col:col + hi - lo] = _proj(hn, wt_ref, lo, hi)
        return emit

    all_pieces = lambda: _proj_pieces(row0, row0 + inner + conv_dim, SSD_HEADS)

    @pl.when(step == 0)
    def _():
        project(h0_ref[...], all_pieces())(len(all_pieces()))

    @pl.when(step % nt == 0)
    def _():
        xpad_ref[:, 0:8, :] = jnp.zeros((nslab, 8, LANES), F32)
        state_ref[...] = jnp.zeros(state_ref.shape, F32)

    dt_raw = p_ref[:, inner + conv_dim:]
    z_ref[...] = p_ref[:, :inner]
    for j in range(nslab):
        xpad_ref[j, 8:8 + cs, :] = p_ref[:, inner + j * LANES:inner + (j + 1) * LANES]
    emit_next = project(hnext_ref[...], all_pieces())

    convs = []
    for j in range(nslab):
        lanes = slice(j * LANES, (j + 1) * LANES)
        conv = cb_ref[:, lanes]
        for k in range(SSD_CONV):
            lo = 8 - (SSD_CONV - 1) + k
            conv = conv + cw_ref[k:k + 1, lanes] * xpad_ref[j, lo:lo + cs, :]
        xpad_ref[j, 0:8, :] = xpad_ref[j, cs:cs + 8, :]
        convs.append(conv)
        if j % 3 == 2:
            emit_next()
    xc = _silu(jnp.concatenate(convs, axis=1))
    xs = xc[:, :inner]
    bm = xc[:, inner:inner + SSD_GROUPS * SSD_STATE]
    cm = xc[:, inner + SSD_GROUPS * SSD_STATE:]

    head_lane = lax.broadcasted_iota(jnp.int32, (cs, LANES), 1) < SSD_HEADS
    dt = jnp.where(head_lane, _softplus(dt_raw + dtb_ref[...]), 0.0)
    da = dt * -jnp.exp(alog_ref[...])
    a_cum = jnp.where(head_lane, _unpack3(_dot(tril_ref[...], _pack3(da))), 0.0)
    a_cum_t = a_cum.T
    expand = expand_ref[...]
    xdt = xs * _dot(_pack3(dt), expand)
    a_x = _dot(_pack3(a_cum), expand)
    a_last = a_x[cs - 1:cs, :]
    decay_in = jnp.exp(a_x)
    xd = (xdt * jnp.exp(a_last - a_x)).astype(BF16)
    xdt = xdt.astype(BF16)

    state = state_ref[...]
    y = _dot(cm.astype(BF16), state.astype(BF16)) * decay_in
    new_state = state * jnp.exp(a_last) + _dot(bm.T.astype(BF16), xd)
    state_ref[...] = new_state * bdmask_ref[...]

    row = lax.broadcasted_iota(jnp.int32, (cs, cs), 0)
    col = lax.broadcasted_iota(jnp.int32, (cs, cs), 1)
    causal = row >= col
    bm16 = bm.astype(BF16)
    ys = []
    for g in range(SSD_GROUPS):
        cb = _dot_nt((cm * gmask_ref[g:g + 1, :]).astype(BF16), bm16)
        xg = xdt[:, g * gw:(g + 1) * gw]
        for j in range(hpg):
            h = g * hpg + j
            diff = a_cum[:, h:h + 1] - a_cum_t[h:h + 1, :]
            seg = jnp.exp(jnp.where(causal, diff, -jnp.inf))
            lhs_ref[:, j * cs:(j + 1) * cs] = (cb * seg).astype(BF16)
            rhs_ref[j * cs:(j + 1) * cs, :] = xg * hmask_ref[j:j + 1, :]
        ys.append(_dot(lhs_ref[...], rhs_ref[...]))
        emit_next(2)
    y = y + jnp.concatenate(ys, axis=1) + dsk_ref[...] * xs
    y = y * _silu(z_ref[...])
    o_ref[...] = _rms(y, ng_ref[...]).astype(o_ref.dtype)
    emit_next(len(all_pieces()))


def ssd_mixer(h, h_norm_g, wt_in, row0, conv_w, conv_b, dt_bias, a_log, d_skip, norm_g, batch, seqlen, riders=()):
    cs = SSD_TILE
    d_model = h.shape[1]
    inner = SSD_HEADS * SSD_HEADDIM
    conv_dim = conv_w.shape[1]
    gw = inner // SSD_GROUPS
    gs = SSD_GROUPS * SSD_STATE
    hpg = SSD_HEADS // SSD_GROUPS
    nt = seqlen // cs
    assert seqlen % cs == 0 and h.shape[0] == batch * seqlen and conv_dim % LANES == 0
    assert wt_in.shape[1] == d_model and row0 + inner + conv_dim + SSD_HEADS <= wt_in.shape[0]

    pad = lambda v: jnp.pad(v.astype(F32), (0, LANES - SSD_HEADS)).reshape(1, LANES)
    assert SSD_HEADS == PACK
    lane_head = np.arange(inner) // SSD_HEADDIM
    packed_head = np.where(np.arange(LANES) < 3 * PACK, np.arange(LANES) % PACK, -1)
    expand = jnp.asarray(packed_head[:, None] == lane_head[None, :], BF16)
    tril = jnp.asarray(np.arange(cs)[:, None] >= np.arange(cs)[None, :], BF16)
    hmask = jnp.asarray(np.arange(hpg)[:, None] == (np.arange(gw) // SSD_HEADDIM)[None, :], BF16)
    gmask = jnp.asarray(np.arange(SSD_GROUPS)[:, None] == (np.arange(gs) // SSD_STATE)[None, :], F32)
    bdmask = jnp.asarray((np.arange(gs) // SSD_STATE)[:, None] == (np.arange(inner) // gw)[None, :], F32)

    ntiles = batch * nt
    consts = [h_norm_g.astype(F32).reshape(1, d_model), wt_in,
              conv_w.astype(F32), conv_b.astype(F32).reshape(1, -1), pad(dt_bias), pad(a_log),
              jnp.repeat(d_skip.astype(F32), SSD_HEADDIM).reshape(1, inner), norm_g.astype(F32).reshape(1, inner),
              expand, tril, hmask, gmask, bdmask]
    rider_in, rider_out, rider_shapes = _rider_specs(riders)
    body = functools.partial(_ssd_kernel, row0=row0, nt=nt)
    return pl.pallas_call(
        _with_riders(body, 2 + len(consts), 1, riders), grid=(ntiles,),
        in_specs=[pl.BlockSpec((cs, d_model), lambda s: (0, 0)),
                  pl.BlockSpec((cs, d_model), lambda s: (jnp.minimum(s + 1, ntiles - 1), 0))]
        + [_resident(c.shape) for c in consts] + rider_in,
        out_specs=[pl.BlockSpec((cs, inner), lambda s: (s, 0))] + rider_out,
        out_shape=[jax.ShapeDtypeStruct((batch * seqlen, inner), BF16)] + rider_shapes,
        scratch_shapes=[pltpu.VMEM((conv_dim // LANES, cs + 8, LANES), F32), pltpu.VMEM((gs, inner), F32),
                        pltpu.VMEM((cs, hpg * cs), BF16), pltpu.VMEM((hpg * cs, gw), BF16),
                        pltpu.VMEM((cs, inner + conv_dim + LANES), F32), pltpu.VMEM((cs, inner), F32)],
        compiler_params=_params("arbitrary"), name="ssd_mixer",
    )(h, h, *consts, *[w for w, _, _ in riders])


S5_TILE = 128
S5_CHUNK = 128
S5_GROUP = 16
S5_STATE = 64
S5_GROUPS_PER_BLOCK = 16


def _s5_kernel(h_ref, ng_ref, wt_ref, wb_ref, a_ref, wc_ref, d_ref, gw_ref, gb_ref, o_ref,
               bu_ref, x_ref, y_ref, abc_ref):
    bsz, tl, d_model = h_ref.shape
    width = d_ref.shape[1]
    nblk = wb_ref.shape[0]
    kw = wb_ref.shape[1]
    sw = a_ref.shape[2]
    rows = tl * bsz

    lanes = lambda kb: slice(kb * 2 * sw, (kb + 1) * 2 * sw)
    re = lambda kb: slice(kb * 2 * sw, kb * 2 * sw + sw)
    im = lambda kb: slice(kb * 2 * sw + sw, (kb + 1) * 2 * sw)

    @pl.when(pl.program_id(0) == 0)
    def _():
        x_ref[...] = jnp.zeros(x_ref.shape, F32)
        for kb in range(nblk):
            abc_ref[:, re(kb)] = jnp.broadcast_to(a_ref[kb, 0:1, :], (bsz, sw))
            abc_ref[:, im(kb)] = jnp.broadcast_to(a_ref[kb, 1:2, :], (bsz, sw))

    hn = _rms(h_ref[...].reshape(rows, d_model), ng_ref[...]).astype(BF16)
    u = _proj(hn, wt_ref, 0, width)
    u = jnp.swapaxes(u.reshape(bsz, tl, width), 0, 1).reshape(rows, width)
    u16 = u.astype(BF16)

    crows = S5_CHUNK * bsz
    nchunk = tl // S5_CHUNK

    def input_stage(j):
        rows_j = slice(j * crows, (j + 1) * crows)
        for kb in range(nblk):
            bu_ref[rows_j, lanes(kb)] = _dot(u16[rows_j, kb * kw:(kb + 1) * kw], wb_ref[kb])

    def scan_stage(j, state):
        for t in range(j * S5_CHUNK, (j + 1) * S5_CHUNK):
            r = slice(t * bsz, (t + 1) * bsz)
            new_state = []
            for kb in range(nblk):
                x_re, x_im = state[kb]
                a_re, a_im = abc_ref[:, re(kb)], abc_ref[:, im(kb)]
                n_re = a_re * x_re - a_im * x_im + bu_ref[r, re(kb)]
                n_im = a_re * x_im + a_im * x_re + bu_ref[r, im(kb)]
                bu_ref[r, re(kb)] = n_re
                bu_ref[r, im(kb)] = n_im
                new_state.append((n_re, n_im))
            state = new_state
        return state

    def output_stage(j):
        rows_j = slice(j * crows, (j + 1) * crows)
        ys = [_dot(bu_ref[rows_j, lanes(kb)].astype(BF16), wc_ref[kb]) for kb in range(nblk)]
        y = jnp.concatenate(ys, axis=1) + d_ref[...] * u[rows_j, :]
        z = jax.nn.gelu(y)
        y_ref[rows_j, :] = z * jax.nn.sigmoid(_dot(z.astype(BF16), gw_ref[...]) + gb_ref[...])

    state = [(x_ref[:, re(kb)], x_ref[:, im(kb)]) for kb in range(nblk)]
    input_stage(0)
    for j in range(nchunk):
        if j + 1 < nchunk:
            input_stage(j + 1)
        state = scan_stage(j, state)
        output_stage(j)
    for kb in range(nblk):
        x_ref[:, re(kb)] = state[kb][0]
        x_ref[:, im(kb)] = state[kb][1]
    o_ref[...] = jnp.swapaxes(y_ref[...].reshape(tl, bsz, width), 0, 1).astype(o_ref.dtype)


def _s5_weights(lam_re, lam_im, log_step, b_re, b_im, c_re, c_im):
    lr, li = lam_re.astype(F32), lam_im.astype(F32)
    step = jnp.exp(log_step.astype(F32))[:, None]
    mag = jnp.exp(lr * step)
    a_re, a_im = mag * jnp.cos(li * step), mag * jnp.sin(li * step)
    den = lr * lr + li * li
    k_re = ((a_re - 1.0) * lr + a_im * li) / den
    k_im = (a_im * lr - (a_re - 1.0) * li) / den
    br, bi = b_re.astype(F32), b_im.astype(F32)
    bb_re = k_re[..., None] * br - k_im[..., None] * bi
    bb_im = k_re[..., None] * bi + k_im[..., None] * br
    groups, nstate, nch = br.shape
    gpb = S5_GROUPS_PER_BLOCK
    nblk = groups // gpb
    eye = jnp.eye(gpb, dtype=F32)

    def in_block(bb):
        t = bb.reshape(nblk, gpb, nstate, nch).transpose(0, 1, 3, 2)
        t = t[:, :, :, None, :] * eye[None, :, None, :, None]
        return t.reshape(nblk, gpb * nch, gpb * nstate)

    def out_block(c):
        t = c.astype(F32).reshape(nblk, gpb, nch, nstate).transpose(0, 1, 3, 2)
        t = t[:, :, :, None, :] * eye[None, :, None, :, None]
        return t.reshape(nblk, gpb * nstate, gpb * nch)

    wb = jnp.concatenate([in_block(bb_re), in_block(bb_im)], axis=2).astype(BF16)
    wc = jnp.concatenate([out_block(c_re), -out_block(c_im)], axis=1).astype(BF16)
    a = jnp.stack([a_re.reshape(nblk, gpb * nstate), a_im.reshape(nblk, gpb * nstate)], axis=1)
    return wb, a, wc


def s5_mixer(h, norm_g, wt_in, lam_re, lam_im, log_step, b_re, b_im, c_re, c_im, d_skip, glu_w, glu_b, riders=()):
    bsz, seqlen, d_model = h.shape
    width = glu_w.shape[0]
    tl = S5_TILE
    assert bsz == SUBLANES and seqlen % tl == 0
    wb, a, wc = _s5_weights(lam_re, lam_im, log_step, b_re, b_im, c_re, c_im)
    nstates = wb.shape[0] * wb.shape[2]
    consts = [norm_g.astype(F32).reshape(1, d_model), wt_in, wb, a, wc, d_skip.astype(F32).reshape(1, width),
              glu_w.astype(BF16), glu_b.astype(F32).reshape(1, width)]
    tok = lambda n: pl.BlockSpec((bsz, tl, n), lambda i: (0, i, 0))
    const_specs = [_resident(c.shape) for c in consts]
    const_specs[1] = pl.BlockSpec((width, d_model), lambda i: (0, 0), pipeline_mode=pl.Buffered(1))
    rider_in, rider_out, rider_shapes = _rider_specs(riders)
    return pl.pallas_call(
        _with_riders(_s5_kernel, 1 + len(consts), 1, riders), grid=(seqlen // tl,),
        in_specs=[tok(d_model)] + const_specs + rider_in,
        out_specs=[tok(width)] + rider_out,
        out_shape=[jax.ShapeDtypeStruct((bsz, seqlen, width), BF16)] + rider_shapes,
        scratch_shapes=[pltpu.VMEM((tl * bsz, nstates), F32), pltpu.VMEM((bsz, nstates), F32),
                        pltpu.VMEM((tl * bsz, width), F32), pltpu.VMEM((bsz, nstates), F32)],
        compiler_params=_params("arbitrary"), name="s5_mixer",
    )(h, *consts, *[w for w, _, _ in riders])


MIX_TILE = 256
MIX_HEADS = 4
MIX_QK = 128
MIX_V = 256
ROPE_BASE = 10000.0


def _rotary(t, cos, sin_signed):
    return t * cos + pltpu.roll(t, MIX_QK // 2, axis=1) * sin_signed


def _project_head(hn, wt_ref, p_ref, row0, g_row0, h):
    kw, vw = MIX_HEADS * MIX_QK, MIX_HEADS * MIX_V
    q_lo, k_lo, v_lo = h * MIX_QK, kw + h * MIX_QK, 2 * kw + h * MIX_V
    w_qk = jnp.concatenate([wt_ref[row0 + q_lo:row0 + q_lo + MIX_QK, :],
                            wt_ref[row0 + k_lo:row0 + k_lo + MIX_QK, :]], axis=0)
    qk = _dot_nt(hn, w_qk)
    p_ref[:, q_lo:q_lo + MIX_QK] = qk[:, :MIX_QK]
    p_ref[:, k_lo:k_lo + MIX_QK] = qk[:, MIX_QK:]
    p_ref[:, v_lo:v_lo + MIX_V] = _proj(hn, wt_ref, row0 + v_lo, row0 + v_lo + MIX_V)
    g_lo = h * MIX_V
    p_ref[:, 2 * kw + vw + g_lo:2 * kw + vw + g_lo + MIX_V] = _proj(hn, wt_ref, g_row0 + g_lo, g_row0 + g_lo + MIX_V)


def _ret_kernel(hn0_ref, hnext_ref, wt_ref, cos_ref, sin_ref, dmat_ref, qdec_ref, kdec_ref, cdec_ref,
                ng_ref, o_ref, state_ref, p_ref, *, row0, nt):
    step = pl.program_id(0)
    kw, vw = MIX_HEADS * MIX_QK, MIX_HEADS * MIX_V
    g_row0 = row0 + 2 * kw + vw

    @pl.when(step == 0)
    def _():
        for h in range(MIX_HEADS):
            _project_head(hn0_ref[...], wt_ref, p_ref, row0, g_row0, h)

    @pl.when(step % nt == 0)
    def _():
        state_ref[...] = jnp.zeros(state_ref.shape, F32)

    hn_next = hnext_ref[...]
    cos, sin = cos_ref[...], sin_ref[...]
    scale = MIX_QK ** -0.5
    for h in range(MIX_HEADS):
        vl = slice(h * MIX_V, (h + 1) * MIX_V)
        q = _rotary(p_ref[:, h * MIX_QK:(h + 1) * MIX_QK], cos, sin)
        k = _rotary(p_ref[:, kw + h * MIX_QK:kw + (h + 1) * MIX_QK], cos, sin) * scale
        v16 = p_ref[:, 2 * kw + h * MIX_V:2 * kw + (h + 1) * MIX_V].astype(BF16)
        gate = _silu(p_ref[:, 2 * kw + vw + h * MIX_V:2 * kw + vw + (h + 1) * MIX_V])
        q16 = q.astype(BF16)
        inner = _dot_nt(q16, k.astype(BF16)) * dmat_ref[h]
        state = state_ref[h]
        out = _dot(inner.astype(BF16), v16) + _dot_nt((q * qdec_ref[h]).astype(BF16), state.astype(BF16))
        state_ref[h] = state * cdec_ref[h] + _dot_tn(v16, (k * kdec_ref[h]).astype(BF16))
        _project_head(hn_next, wt_ref, p_ref, row0, g_row0, h)
        mu = jnp.mean(out, axis=-1, keepdims=True)
        cen = out - mu
        var = jnp.mean(cen * cen, axis=-1, keepdims=True)
        o = cen * lax.rsqrt(var + NORM_EPS) * ng_ref[:, vl]
        o_ref[:, vl] = (o * gate).astype(o_ref.dtype)


def _rope_tables(seqlen):
    half = MIX_QK // 2
    inv_freq = (np.float32(ROPE_BASE) ** (-np.arange(half, dtype=np.float32) / half)).astype(np.float32)
    ang = (np.arange(seqlen, dtype=np.float32)[:, None] * inv_freq[None, :]).astype(np.float64)
    cos, sin = np.cos(ang), np.sin(ang)
    return (jnp.asarray(np.concatenate([cos, cos], axis=1), F32),
            jnp.asarray(np.concatenate([-sin, sin], axis=1), F32))


def retention_mixer(hn, wt_in, row0, norm_g, batch, seqlen):
    c = MIX_TILE
    nt = seqlen // c
    d_model = hn.shape[1]
    kw, vw = MIX_HEADS * MIX_QK, MIX_HEADS * MIX_V
    assert seqlen % c == 0 and hn.shape[0] == batch * seqlen
    assert wt_in.shape[1] == d_model and row0 + 2 * kw + 2 * vw <= wt_in.shape[0]
    cos, sin = _rope_tables(seqlen)
    log_gamma = np.log(1.0 - 2.0 ** (-5.0 - np.arange(MIX_HEADS, dtype=np.float64)))
    pos = np.arange(c, dtype=np.float64)
    diff = pos[:, None] - pos[None, :]
    dmat = jnp.asarray(np.where(diff >= 0, np.exp(log_gamma[:, None, None] * np.maximum(diff, 0.0)), 0.0), F32)
    lanes = lambda col: jnp.asarray(np.broadcast_to(col[:, :, None], (MIX_HEADS, c, MIX_QK)), F32)
    qdec = lanes(np.exp(log_gamma[:, None] * (pos + 1.0)))
    kdec = lanes(np.exp(log_gamma[:, None] * (c - 1.0 - pos)))
    cdec = jnp.asarray(np.broadcast_to(np.exp(log_gamma * c)[:, None, None], (MIX_HEADS, 1, MIX_QK)), F32)
    consts = [dmat, qdec, kdec, cdec, norm_g.astype(F32).reshape(1, vw)]
    ntiles = batch * nt
    rope = pl.BlockSpec((c, MIX_QK), lambda s: (s % nt, 0))
    return pl.pallas_call(
        functools.partial(_ret_kernel, row0=row0, nt=nt), grid=(ntiles,),
        in_specs=[pl.BlockSpec((c, d_model), lambda s: (0, 0)),
                  pl.BlockSpec((c, d_model), lambda s: (jnp.minimum(s + 1, ntiles - 1), 0)),
                  _resident(wt_in.shape), rope, rope] + [_resident(x.shape) for x in consts],
        out_specs=pl.BlockSpec((c, vw), lambda s: (s, 0)),
        out_shape=jax.ShapeDtypeStruct((batch * seqlen, vw), BF16),
        scratch_shapes=[pltpu.VMEM((MIX_HEADS, MIX_V, MIX_QK), F32), pltpu.VMEM((c, 2 * kw + 2 * vw), F32)],
        compiler_params=_params("arbitrary"), name="retention_mixer",
    )(hn, hn, wt_in, cos, sin, *consts)


GLA_CHUNK = 64
GLA_TAU = 16.0


def _log_sigmoid(x):
    return jnp.minimum(x, 0.0) - jnp.log1p(jnp.exp(-jnp.abs(x)))


def _gla_kernel(hn0_ref, hnext_ref, wt_ref, gw_ref, gb_ref, ng_ref, cumsel_ref,
                o_ref, state_ref, p_ref, *, row0, rank, nt):
    tl = hnext_ref.shape[0]
    nchunk = tl // GLA_CHUNK
    kw, vw = MIX_HEADS * MIX_QK, MIX_HEADS * MIX_V
    step = pl.program_id(0)
    a_row0 = row0 + 2 * kw + vw
    g_row0 = a_row0 + rank
    a_col0 = 2 * kw + 2 * vw

    @pl.when(step == 0)
    def _():
        hn0 = hn0_ref[...]
        p_ref[:, a_col0:] = _proj_narrow(hn0, wt_ref, a_row0, a_row0 + rank)
        for h in range(MIX_HEADS):
            _project_head(hn0, wt_ref, p_ref, row0, g_row0, h)

    @pl.when(step % nt == 0)
    def _():
        state_ref[...] = jnp.zeros(state_ref.shape, F32)

    hn_next = hnext_ref[...]
    a_lr = p_ref[:, a_col0:]
    log_a = _log_sigmoid(_dot(a_lr.astype(BF16), gw_ref[...]) + gb_ref[...]) / GLA_TAU
    p_ref[:, a_col0:] = _proj_narrow(hn_next, wt_ref, a_row0, a_row0 + rank)
    hi, mid, lo = _split3(log_a)
    zeros = jnp.zeros((GLA_CHUNK, kw), BF16)
    bcums, blasts = [], []
    for c in range(nchunk):
        rows = slice(c * GLA_CHUNK, (c + 1) * GLA_CHUNK)
        terms = jnp.concatenate([hi[rows, :], mid[rows, :], lo[rows, :], zeros], axis=0)
        bc = _dot(cumsel_ref[...], terms)
        bcums.append(bc)
        blasts.append(jnp.broadcast_to(bc[GLA_CHUNK - 1:GLA_CHUNK, :], (GLA_CHUNK, kw)))
    bcum = jnp.concatenate(bcums, axis=0)
    blast = jnp.concatenate(blasts, axis=0)
    grow = jnp.exp(bcum)
    row = lax.broadcasted_iota(jnp.int32, (tl, tl), 0)
    col = lax.broadcasted_iota(jnp.int32, (tl, tl), 1)
    shift = GLA_CHUNK.bit_length() - 1
    keep = (row >= col) & (jnp.right_shift(row, shift) == jnp.right_shift(col, shift))
    bcum_t = bcum.T
    scale = MIX_QK ** -0.5
    for h in range(MIX_HEADS):
        kl = slice(h * MIX_QK, (h + 1) * MIX_QK)
        vl = slice(h * MIX_V, (h + 1) * MIX_V)
        k = p_ref[:, kw + h * MIX_QK:kw + (h + 1) * MIX_QK]
        q_in = (p_ref[:, kl] * scale * grow[:, kl]).astype(BF16)
        k_in = (k * jnp.exp(-bcum[:, kl])).astype(BF16)
        k_out = (k * jnp.exp(blast[:, kl] - bcum[:, kl])).astype(BF16)
        v = p_ref[:, 2 * kw + h * MIX_V:2 * kw + (h + 1) * MIX_V]
        gate = _silu(p_ref[:, 2 * kw + vw + h * MIX_V:2 * kw + vw + (h + 1) * MIX_V])
        v16 = v.astype(BF16)
        att = jnp.where(keep, _dot_nt(q_in, k_in), 0.0)
        out = _dot(att.astype(BF16), v16)
        _project_head(hn_next, wt_ref, p_ref, row0, g_row0, h)
        chunks = [slice(c * GLA_CHUNK, (c + 1) * GLA_CHUNK) for c in range(nchunk)]
        kvs = [_dot_tn(k_out[rows, :], v16[rows, :]) for rows in chunks]
        decays = [jnp.exp(bcum_t[h * MIX_QK:(h + 1) * MIX_QK, rows.stop - 1:rows.stop]) for rows in chunks]
        state = state_ref[h]
        inter = []
        for rows, kv, decay in zip(chunks, kvs, decays):
            inter.append(_dot(q_in[rows, :], state.astype(BF16)))
            state = state * decay + kv
        state_ref[h] = state
        out = out + jnp.concatenate(inter, axis=0)
        o = out * lax.rsqrt(jnp.mean(out * out, axis=-1, keepdims=True) + NORM_EPS) * ng_ref[:, vl]
        o_ref[:, vl] = (o * gate).astype(o_ref.dtype)


def gla_mixer(hn, wt_in, row0, gate_w, gate_b, norm_g, batch, seqlen):
    tl = MIX_TILE
    nt = seqlen // tl
    d_model = hn.shape[1]
    kw, vw = MIX_HEADS * MIX_QK, MIX_HEADS * MIX_V
    assert seqlen % tl == 0 and tl % GLA_CHUNK == 0 and hn.shape[0] == batch * seqlen
    rank = gate_w.shape[0]
    assert wt_in.shape[1] == d_model and row0 + 2 * kw + 2 * vw + rank <= wt_in.shape[0]
    gw = jnp.pad(gate_w.astype(F32), ((0, LANES - rank), (0, 0))).astype(BF16)
    t = np.arange(GLA_CHUNK)
    tril = t[:, None] >= t[None, :]
    cumsel = jnp.asarray(np.concatenate([tril, tril, tril, np.zeros_like(tril)], axis=1), BF16)
    consts = [gw, gate_b.astype(F32).reshape(1, kw), norm_g.astype(F32).reshape(1, vw), cumsel]
    ntiles = batch * nt
    return pl.pallas_call(
        functools.partial(_gla_kernel, row0=row0, rank=rank, nt=nt), grid=(ntiles,),
        in_specs=[pl.BlockSpec((tl, d_model), lambda s: (0, 0)),
                  pl.BlockSpec((tl, d_model), lambda s: (jnp.minimum(s + 1, ntiles - 1), 0)),
                  _resident(wt_in.shape)] + [_resident(x.shape) for x in consts],
        out_specs=pl.BlockSpec((tl, vw), lambda s: (s, 0)),
        out_shape=jax.ShapeDtypeStruct((batch * seqlen, vw), BF16),
        scratch_shapes=[pltpu.VMEM((MIX_HEADS, MIX_QK, MIX_V), F32),
                        pltpu.VMEM((tl, 2 * kw + 2 * vw + LANES), F32)],
        compiler_params=_params("arbitrary"), name="gla_mixer",
    )(hn, hn, wt_in, *consts)


def _row_blocks(rows, max_blocks):
    return max(n for n in range(1, max_blocks + 1) if rows % n == 0 and (rows // n) % BF16_ROWS == 0)


def kernel(x, norm_mix_g, norm_ffn_g, final_norm_g, ev_in_w, s5_lam_re, s5_lam_im, s5_log_step, s5_b_re, s5_b_im, s5_c_re, s5_c_im, s5_d, s5_glu_w, s5_glu_b, ssd_conv_w, ssd_conv_b, ssd_dt_bias, ssd_a_log, ssd_d, ssd_norm_g, ev_out_w, od_in_w, ret_norm_g, gla_gate_w, gla_gate_b, gla_norm_g, od_out_w, ffn_gate_w, ffn_up_w, ffn_down_w):
    bsz, seqlen, d = x.shape
    rows = bsz * seqlen
    depth = norm_mix_g.shape[0]
    assert depth == 2 and ev_in_w.shape[0] == 1 and od_in_w.shape[0] == 1, "even layer then odd layer"
    h = x.reshape(rows, d)
    kw, vw = gla_gate_w.shape[2], ret_norm_g.shape[1]
    s5_w = s5_glu_w.shape[1]
    hidden = ffn_gate_w.shape[2]

    wt_even = jnp.swapaxes(ev_in_w[0], 0, 1).astype(BF16)
    s5_steps, ssd_steps = seqlen // S5_TILE, rows // SSD_TILE
    y_a, wg0, wu0, wd0, wo0 = s5_mixer(
        h.reshape(bsz, seqlen, d), norm_mix_g[0], wt_even, s5_lam_re[0], s5_lam_im[0], s5_log_step[0],
        s5_b_re[0], s5_b_im[0], s5_c_re[0], s5_c_im[0], s5_d[0].reshape(-1), s5_glu_w[0], s5_glu_b[0],
        riders=((ffn_gate_w, (0,), s5_steps), (ffn_up_w, (0,), s5_steps), (ffn_down_w, (0,), s5_steps),
                (ev_out_w, (0,), s5_steps)))
    wt_odd_f32 = jnp.swapaxes(od_in_w[0], 0, 1)
    y_b, wg1, wu1, wd1, wo1, wt_odd = ssd_mixer(
        h, norm_mix_g[0], wt_even, s5_w, ssd_conv_w[0], ssd_conv_b[0], ssd_dt_bias[0], ssd_a_log[0],
        ssd_d[0], ssd_norm_g[0], bsz, seqlen,
        riders=((ffn_gate_w, (1,), ssd_steps), (ffn_up_w, (1,), ssd_steps),
                (ffn_down_w, (1,), _row_blocks(hidden, ssd_steps)), (od_out_w, (0,), ssd_steps),
                (wt_odd_f32, (), _row_blocks(wt_odd_f32.shape[0], ssd_steps))))
    h, hn = mix_ffn(h, y_a.reshape(rows, s5_w), y_b, wo0, norm_ffn_g[0], wg0, wu0, wd0, norm_mix_g[1],
                    final_norm=False)

    y_a = retention_mixer(hn, wt_odd, 0, ret_norm_g[0], bsz, seqlen)
    y_b = gla_mixer(hn, wt_odd, 2 * kw + 2 * vw, gla_gate_w[0], gla_gate_b[0], gla_norm_g[0], bsz, seqlen)
    h = mix_ffn(h, y_a, y_b, wo1, norm_ffn_g[1], wg1, wu1, wd1, final_norm_g, final_norm=True)
    return h.reshape(bsz, seqlen, d)
```

```python
import functools

import jax
import jax.numpy as jnp
import numpy as np
from jax import lax
from jax.experimental import pallas as pl
from jax.experimental.pallas import tpu as pltpu

F32 = jnp.float32
BF16 = jnp.bfloat16

NORM_EPS = 1e-6
VMEM_LIMIT_BYTES = 56 * 1024 * 1024
LANES = 128
SUBLANES = 8

ROW_TILE = 512
FFN_CHUNK = 256


def _rms(x, g):
    return x * lax.rsqrt(jnp.mean(x * x, axis=-1, keepdims=True) + NORM_EPS) * g


def _dot(a, b):
    return jnp.dot(a, b, preferred_element_type=F32)


def _resident(shape):
    zeros = (0,) * len(shape)
    return pl.BlockSpec(shape, lambda *_: zeros, pipeline_mode=pl.Buffered(1))


def _params(*semantics):
    return pltpu.CompilerParams(dimension_semantics=semantics,
                                vmem_limit_bytes=VMEM_LIMIT_BYTES)


BF16_ROWS = 16


def _rider_specs(riders):
    in_specs, out_specs, out_shapes = [], [], []
    for w, prefix, nblocks in riders:
        rows, cols = w.shape[len(prefix):]
        rb = rows // nblocks
        assert rows % nblocks == 0 and rb % BF16_ROWS == 0 and w.dtype == F32
        in_specs.append(pl.BlockSpec(
            (None,) * len(prefix) + (rb, cols),
            lambda s, prefix=prefix, nblocks=nblocks: prefix + (jnp.minimum(s, nblocks - 1), 0)))
        out_specs.append(pl.BlockSpec((rb, cols), lambda s, nblocks=nblocks: (jnp.minimum(s, nblocks - 1), 0)))
        out_shapes.append(jax.ShapeDtypeStruct((rows, cols), BF16))
    return in_specs, out_specs, out_shapes


def _with_riders(body, n_in, n_out, riders):
    nr = len(riders)

    def kernel(*refs):
        ins, rider_ins = refs[:n_in], refs[n_in:n_in + nr]
        outs = refs[n_in + nr:n_in + nr + n_out]
        rider_outs = refs[n_in + nr + n_out:n_in + 2 * nr + n_out]
        scratch = refs[n_in + 2 * nr + n_out:]
        for (_, _, nblocks), w_ref, o_ref in zip(riders, rider_ins, rider_outs):
            @pl.when(pl.program_id(0) < nblocks)
            def _(w_ref=w_ref, o_ref=o_ref):
                o_ref[...] = w_ref[...].astype(o_ref.dtype)
        body(*ins, *outs, *scratch)

    return kernel


def _mix_ffn_kernel(h_ref, ya_ref, yb_ref, wo_ref, g_ref, wg_ref, wu_ref, wd_ref,
                    ng_ref, *refs, final_norm):
    act_ref = refs[-1]
    na = ya_ref.shape[1]
    h1 = h_ref[...] + _dot(ya_ref[...], wo_ref[:na, :]) + _dot(yb_ref[...], wo_ref[na:, :])
    xn = _rms(h1, g_ref[...]).astype(BF16)
    hidden = wg_ref.shape[1]
    for j in range(hidden // FFN_CHUNK):
        cols = slice(j * FFN_CHUNK, (j + 1) * FFN_CHUNK)
        gate = _dot(xn, wg_ref[:, cols])
        up = _dot(xn, wu_ref[:, cols])
        act_ref[:, cols] = (gate * jax.nn.sigmoid(gate) * up).astype(BF16)
    out = h1 + _dot(act_ref[...], wd_ref[...])
    normed = _rms(out, ng_ref[...])
    if final_norm:
        refs[0][...] = normed
    else:
        refs[0][...] = out
        refs[1][...] = normed.astype(BF16)


def mix_ffn(h, ya, yb, wo, g, wg, wu, wd, next_g, final_norm, row_tile=ROW_TILE):
    rows, d = h.shape
    hidden = wg.shape[1]
    assert rows % row_tile == 0 and hidden % FFN_CHUNK == 0
    assert wo.shape == (ya.shape[1] + yb.shape[1], d)
    row = lambda n: pl.BlockSpec((row_tile, n), lambda i: (i, 0))
    out_shape = [jax.ShapeDtypeStruct((rows, d), F32)]
    if not final_norm:
        out_shape.append(jax.ShapeDtypeStruct((rows, d), BF16))
    outs = pl.pallas_call(
        functools.partial(_mix_ffn_kernel, final_norm=final_norm),
        grid=(rows // row_tile,),
        in_specs=[row(d), row(ya.shape[1]), row(yb.shape[1]), _resident(wo.shape), _resident((1, d)),
                  _resident(wg.shape), _resident(wu.shape), _resident(wd.shape), _resident((1, d))],
        out_specs=[row(d)] * len(out_shape),
        out_shape=out_shape,
        scratch_shapes=[pltpu.VMEM((row_tile, hidden), BF16)],
        compiler_params=_params("parallel"), name="mix_ffn",
    )(h, ya, yb, wo, g.reshape(1, d), wg, wu, wd, next_g.reshape(1, d))
    return outs[0] if final_norm else tuple(outs)


def _split3(a):
    hi = a.astype(BF16)
    r1 = a - hi.astype(F32)
    mid = r1.astype(BF16)
    lo = (r1 - mid.astype(F32)).astype(BF16)
    return hi, mid, lo


PACK = 16


def _pack3(a):
    hi, mid, lo = _split3(a)
    packed = hi.astype(F32) + pltpu.roll(mid.astype(F32), PACK, 1) + pltpu.roll(lo.astype(F32), 2 * PACK, 1)
    return packed.astype(BF16)


def _unpack3(r):
    return r + pltpu.roll(r, LANES - PACK, 1) + pltpu.roll(r, LANES - 2 * PACK, 1)


def _dot_nt(a, b):
    return lax.dot_general(a, b, (((1,), (1,)), ((), ())), preferred_element_type=F32)


def _dot_tn(a, b):
    return lax.dot_general(a, b, (((0,), (0,)), ((), ())), preferred_element_type=F32)


def _proj(hn, wt_ref, lo, hi):
    return _dot_nt(hn, wt_ref[lo:hi, :])


def _proj_narrow(hn, wt_ref, lo, hi):
    pad = jnp.zeros((LANES - (hi - lo), wt_ref.shape[1]), wt_ref.dtype)
    return _dot_nt(hn, jnp.concatenate([wt_ref[lo:hi, :], pad], axis=0))


def _silu(x):
    return x * jax.nn.sigmoid(x)


def _softplus(x):
    return jnp.maximum(x, 0.0) + jnp.log1p(jnp.exp(-jnp.abs(x)))


SSD_TILE = 256
SSD_HEADS = 16
SSD_HEADDIM = 64
SSD_GROUPS = 4
SSD_STATE = 64
SSD_CONV = 4
PROJ_PIECE = 256


def _proj_pieces(lo, hi, narrow_rows):
    pieces = [(a, min(a + PROJ_PIECE, hi), False) for a in range(lo, hi, PROJ_PIECE)]
    return pieces + [(hi, hi + narrow_rows, True)]


def _ssd_kernel(h0_ref, hnext_ref, hg_ref, wt_ref, cw_ref, cb_ref, dtb_ref, alog_ref, dsk_ref, ng_ref,
                expand_ref, tril_ref, hmask_ref, gmask_ref, bdmask_ref,
                o_ref, xpad_ref, state_ref, lhs_ref, rhs_ref, p_ref, z_ref, *, row0, nt):
    cs = hnext_ref.shape[0]
    inner = o_ref.shape[1]
    gw = inner // SSD_GROUPS
    hpg = SSD_HEADS // SSD_GROUPS
    nslab = xpad_ref.shape[0]
    conv_dim = nslab * LANES
    step = pl.program_id(0)

    def project(h_tile, pieces):
        hn = _rms(h_tile, hg_ref[...]).astype(BF16)

        def emit(count=1):
            for _ in range(min(count, len(pieces))):
                lo, hi, narrow = pieces.pop(0)
                col = lo - row0
                if narrow:
                    p_ref[:, col:col + LANES] = _proj_narrow(hn, wt_ref, lo, hi)
                else:
                    p_ref[:, col:col + hi - lo] = _proj(hn, wt_ref, lo, hi)
        return emit

    all_pieces = lambda: _proj_pieces(row0, row0 + inner + conv_dim, SSD_HEADS)

    @pl.when(step == 0)
    def _():
        project(h0_ref[...], all_pieces())(len(all_pieces()))

    @pl.when(step % nt == 0)
    def _():
        xpad_ref[:, 0:8, :] = jnp.zeros((nslab, 8, LANES), F32)
        state_ref[...] = jnp.zeros(state_ref.shape, F32)

    dt_raw = p_ref[:, inner + conv_dim:]
    z_ref[...] = p_ref[:, :inner]
    for j in range(nslab):
        xpad_ref[j, 8:8 + cs, :] = p_ref[:, inner + j * LANES:inner + (j + 1) * LANES]
    emit_next = project(hnext_ref[...], all_pieces())

    convs = []
    for j in range(nslab):
        lanes = slice(j * LANES, (j + 1) * LANES)
        conv = cb_ref[:, lanes]
        for k in range(SSD_CONV):
            lo = 8 - (SSD_CONV - 1) + k
            conv = conv + cw_ref[k:k + 1, lanes] * xpad_ref[j, lo:lo + cs, :]
        xpad_ref[j, 0:8, :] = xpad_ref[j, cs:cs + 8, :]
        convs.append(conv)
        if j % 3 == 2:
            emit_next()
    xc = _silu(jnp.concatenate(convs, axis=1))
    xs = xc[:, :inner]
    bm = xc[:, inner:inner + SSD_GROUPS * SSD_STATE]
    cm = xc[:, inner + SSD_GROUPS * SSD_STATE:]

    head_lane = lax.broadcasted_iota(jnp.int32, (cs, LANES), 1) < SSD_HEADS
    dt = jnp.where(head_lane, _softplus(dt_raw + dtb_ref[...]), 0.0)
    da = dt * -jnp.exp(alog_ref[...])
    a_cum = jnp.where(head_lane, _unpack3(_dot(tril_ref[...], _pack3(da))), 0.0)
    a_cum_t = a_cum.T
    expand = expand_ref[...]
    xdt = xs * _dot(_pack3(dt), expand)
    a_x = _dot(_pack3(a_cum), expand)
    a_last = a_x[cs - 1:cs, :]
    decay_in = jnp.exp(a_x)
    xd = (xdt * jnp.exp(a_last - a_x)).astype(BF16)
    xdt = xdt.astype(BF16)

    state = state_ref[...]
    y = _dot(cm.astype(BF16), state.astype(BF16)) * decay_in
    new_state = state * jnp.exp(a_last) + _dot(bm.T.astype(BF16), xd)
    state_ref[...] = new_state * bdmask_ref[...]

    row = lax.broadcasted_iota(jnp.int32, (cs, cs), 0)
    col = lax.broadcasted_iota(jnp.int32, (cs, cs), 1)
    causal = row >= col
    bm16 = bm.astype(BF16)
    ys = []
    for g in range(SSD_GROUPS):
        cb = _dot_nt((cm * gmask_ref[g:g + 1, :]).astype(BF16), bm16)
        xg = xdt[:, g * gw:(g + 1) * gw]
        for j in range(hpg):
            h = g * hpg + j
            diff = a_cum[:, h:h + 1] - a_cum_t[h:h + 1, :]
            seg = jnp.exp(jnp.where(causal, diff, -jnp.inf))
            lhs_ref[:, j * cs:(j + 1) * cs] = (cb * seg).astype(BF16)
            rhs_ref[j * cs:(j + 1) * cs, :] = xg * hmask_ref[j:j + 1, :]
        ys.append(_dot(lhs_ref[...], rhs_ref[...]))
        emit_next(2)
    y = y + jnp.concatenate(ys, axis=1) + dsk_ref[...] * xs
    y = y * _silu(z_ref[...])
    o_ref[...] = _rms(y, ng_ref[...]).astype(o_ref.dtype)
    emit_next(len(all_pieces()))


def ssd_mixer(h, h_norm_g, wt_in, row0, conv_w, conv_b, dt_bias, a_log, d_skip, norm_g, batch, seqlen, riders=()):
    cs = SSD_TILE
    d_model = h.shape[1]
    inner = SSD_HEADS * SSD_HEADDIM
    conv_dim = conv_w.shape[1]
    gw = inner // SSD_GROUPS
    gs = SSD_GROUPS * SSD_STATE
    hpg = SSD_HEADS // SSD_GROUPS
    nt = seqlen // cs
    assert seqlen % cs == 0 and h.shape[0] == batch * seqlen and conv_dim % LANES == 0
    assert wt_in.shape[1] == d_model and row0 + inner + conv_dim + SSD_HEADS <= wt_in.shape[0]

    pad = lambda v: jnp.pad(v.astype(F32), (0, LANES - SSD_HEADS)).reshape(1, LANES)
    assert SSD_HEADS == PACK
    lane_head = np.arange(inner) // SSD_HEADDIM
    packed_head = np.where(np.arange(LANES) < 3 * PACK, np.arange(LANES) % PACK, -1)
    expand = jnp.asarray(packed_head[:, None] == lane_head[None, :], BF16)
    tril = jnp.asarray(np.arange(cs)[:, None] >= np.arange(cs)[None, :], BF16)
    hmask = jnp.asarray(np.arange(hpg)[:, None] == (np.arange(gw) // SSD_HEADDIM)[None, :], BF16)
    gmask = jnp.asarray(np.arange(SSD_GROUPS)[:, None] == (np.arange(gs) // SSD_STATE)[None, :], F32)
    bdmask = jnp.asarray((np.arange(gs) // SSD_STATE)[:, None] == (np.arange(inner) // gw)[None, :], F32)

    ntiles = batch * nt
    consts = [h_norm_g.astype(F32).reshape(1, d_model), wt_in,
              conv_w.astype(F32), conv_b.astype(F32).reshape(1, -1), pad(dt_bias), pad(a_log),
              jnp.repeat(d_skip.astype(F32), SSD_HEADDIM).reshape(1, inner), norm_g.astype(F32).reshape(1, inner),
              expand, tril, hmask, gmask, bdmask]
    rider_in, rider_out, rider_shapes = _rider_specs(riders)
    body = functools.partial(_ssd_kernel, row0=row0, nt=nt)
    return pl.pallas_call(
        _with_riders(body, 2 + len(consts), 1, riders), grid=(ntiles,),
        in_specs=[pl.BlockSpec((cs, d_model), lambda s: (0, 0)),
                  pl.BlockSpec((cs, d_model), lambda s: (jnp.minimum(s + 1, ntiles - 1), 0))]
        + [_resident(c.shape) for c in consts] + rider_in,
        out_specs=[pl.BlockSpec((cs, inner), lambda s: (s, 0))] + rider_out,
        out_shape=[jax.ShapeDtypeStruct((batch * seqlen, inner), BF16)] + rider_shapes,
        scratch_shapes=[pltpu.VMEM((conv_dim // LANES, cs + 8, LANES), F32), pltpu.VMEM((gs, inner), F32),
                        pltpu.VMEM((cs, hpg * cs), BF16), pltpu.VMEM((hpg * cs, gw), BF16),
                        pltpu.VMEM((cs, inner + conv_dim + LANES), F32), pltpu.VMEM((cs, inner), F32)],
        compiler_params=_params("arbitrary"), name="ssd_mixer",
    )(h, h, *consts, *[w for w, _, _ in riders])


S5_TILE = 128
S5_GROUP = 16
S5_STATE = 64
S5_GROUPS_PER_BLOCK = 16


def _s5_kernel(h_ref, ng_ref, wt_ref, wb_ref, a_ref, wc_ref, d_ref, gw_ref, gb_ref, o_ref,
               bu_ref, x_ref, abc_ref):
    bsz, tl, d_model = h_ref.shape
    width = d_ref.shape[1]
    nblk = wb_ref.shape[0]
    kw = wb_ref.shape[1]
    sw = a_ref.shape[2]
    rows = tl * bsz

    lanes = lambda kb: slice(kb * 2 * sw, (kb + 1) * 2 * sw)
    re = lambda kb: slice(kb * 2 * sw, kb * 2 * sw + sw)
    im = lambda kb: slice(kb * 2 * sw + sw, (kb + 1) * 2 * sw)

    @pl.when(pl.program_id(0) == 0)
    def _():
        x_ref[...] = jnp.zeros(x_ref.shape, F32)
        for kb in range(nblk):
            abc_ref[:, re(kb)] = jnp.broadcast_to(a_ref[kb, 0:1, :], (bsz, sw))
            abc_ref[:, im(kb)] = jnp.broadcast_to(a_ref[kb, 1:2, :], (bsz, sw))

    hn = _rms(h_ref[...].reshape(rows, d_model), ng_ref[...]).astype(BF16)
    u = _proj(hn, wt_ref, 0, width)
    u = jnp.swapaxes(u.reshape(bsz, tl, width), 0, 1).reshape(rows, width)
    u16 = u.astype(BF16)

    for kb in range(nblk):
        bu_ref[:, lanes(kb)] = _dot(u16[:, kb * kw:(kb + 1) * kw], wb_ref[kb])

    state = [(x_ref[:, re(kb)], x_ref[:, im(kb)]) for kb in range(nblk)]
    for t in range(tl):
        r = slice(t * bsz, (t + 1) * bsz)
        for kb in range(nblk):
            x_re, x_im = state[kb]
            a_re, a_im = abc_ref[:, re(kb)], abc_ref[:, im(kb)]
            n_re = a_re * x_re - a_im * x_im + bu_ref[r, re(kb)]
            n_im = a_re * x_im + a_im * x_re + bu_ref[r, im(kb)]
            bu_ref[r, re(kb)] = n_re
            bu_ref[r, im(kb)] = n_im
            state[kb] = (n_re, n_im)
    for kb in range(nblk):
        x_ref[:, re(kb)] = state[kb][0]
        x_ref[:, im(kb)] = state[kb][1]

    ys = [_dot(bu_ref[:, lanes(kb)].astype(BF16), wc_ref[kb]) for kb in range(nblk)]
    y = jnp.concatenate(ys, axis=1) + d_ref[...] * u
    z = jax.nn.gelu(y)
    out = z * jax.nn.sigmoid(_dot(z.astype(BF16), gw_ref[...]) + gb_ref[...])
    o_ref[...] = jnp.swapaxes(out.reshape(tl, bsz, width), 0, 1).astype(o_ref.dtype)


def _s5_weights(lam_re, lam_im, log_step, b_re, b_im, c_re, c_im):
    lr, li = lam_re.astype(F32), lam_im.astype(F32)
    step = jnp.exp(log_step.astype(F32))[:, None]
    mag = jnp.exp(lr * step)
    a_re, a_im = mag * jnp.cos(li * step), mag * jnp.sin(li * step)
    den = lr * lr + li * li
    k_re = ((a_re - 1.0) * lr + a_im * li) / den
    k_im = (a_im * lr - (a_re - 1.0) * li) / den
    br, bi = b_re.astype(F32), b_im.astype(F32)
    bb_re = k_re[..., None] * br - k_im[..., None] * bi
    bb_im = k_re[..., None] * bi + k_im[..., None] * br
    groups, nstate, nch = br.shape
    gpb = S5_GROUPS_PER_BLOCK
    nblk = groups // gpb
    eye = jnp.eye(gpb, dtype=F32)

    def in_block(bb):
        t = bb.reshape(nblk, gpb, nstate, nch).transpose(0, 1, 3, 2)
        t = t[:, :, :, None, :] * eye[None, :, None, :, None]
        return t.reshape(nblk, gpb * nch, gpb * nstate)

    def out_block(c):
        t = c.astype(F32).reshape(nblk, gpb, nch, nstate).transpose(0, 1, 3, 2)
        t = t[:, :, :, None, :] * eye[None, :, None, :, None]
        return t.reshape(nblk, gpb * nstate, gpb * nch)

    wb = jnp.concatenate([in_block(bb_re), in_block(bb_im)], axis=2).astype(BF16)
    wc = jnp.concatenate([out_block(c_re), -out_block(c_im)], axis=1).astype(BF16)
    a = jnp.stack([a_re.reshape(nblk, gpb * nstate), a_im.reshape(nblk, gpb * nstate)], axis=1)
    return wb, a, wc


def s5_mixer(h, norm_g, wt_in, lam_re, lam_im, log_step, b_re, b_im, c_re, c_im, d_skip, glu_w, glu_b, riders=()):
    bsz, seqlen, d_model = h.shape
    width = glu_w.shape[0]
    tl = S5_TILE
    assert bsz == SUBLANES and seqlen % tl == 0
    wb, a, wc = _s5_weights(lam_re, lam_im, log_step, b_re, b_im, c_re, c_im)
    nstates = wb.shape[0] * wb.shape[2]
    consts = [norm_g.astype(F32).reshape(1, d_model), wt_in, wb, a, wc, d_skip.astype(F32).reshape(1, width),
              glu_w.astype(BF16), glu_b.astype(F32).reshape(1, width)]
    tok = lambda n: pl.BlockSpec((bsz, tl, n), lambda i: (0, i, 0))
    const_specs = [_resident(c.shape) for c in consts]
    const_specs[1] = pl.BlockSpec((width, d_model), lambda i: (0, 0), pipeline_mode=pl.Buffered(1))
    rider_in, rider_out, rider_shapes = _rider_specs(riders)
    return pl.pallas_call(
        _with_riders(_s5_kernel, 1 + len(consts), 1, riders), grid=(seqlen // tl,),
        in_specs=[tok(d_model)] + const_specs + rider_in,
        out_specs=[tok(width)] + rider_out,
        out_shape=[jax.ShapeDtypeStruct((bsz, seqlen, width), BF16)] + rider_shapes,
        scratch_shapes=[pltpu.VMEM((tl * bsz, nstates), F32), pltpu.VMEM((bsz, nstates), F32),
                        pltpu.VMEM((bsz, nstates), F32)],
        compiler_params=_params("arbitrary"), name="s5_mixer",
    )(h, *consts, *[w for w, _, _ in riders])


MIX_TILE = 256
MIX_HEADS = 4
MIX_QK = 128
MIX_V = 256
ROPE_BASE = 10000.0


def _rotary(t, cos, sin_signed):
    return t * cos + pltpu.roll(t, MIX_QK // 2, axis=1) * sin_signed


def _project_head(hn, wt_ref, p_ref, row0, g_row0, h):
    kw, vw = MIX_HEADS * MIX_QK, MIX_HEADS * MIX_V
    q_lo, k_lo, v_lo = h * MIX_QK, kw + h * MIX_QK, 2 * kw + h * MIX_V
    w_qk = jnp.concatenate([wt_ref[row0 + q_lo:row0 + q_lo + MIX_QK, :],
                            wt_ref[row0 + k_lo:row0 + k_lo + MIX_QK, :]], axis=0)
    qk = _dot_nt(hn, w_qk)
    p_ref[:, q_lo:q_lo + MIX_QK] = qk[:, :MIX_QK]
    p_ref[:, k_lo:k_lo + MIX_QK] = qk[:, MIX_QK:]
    p_ref[:, v_lo:v_lo + MIX_V] = _proj(hn, wt_ref, row0 + v_lo, row0 + v_lo + MIX_V)
    g_lo = h * MIX_V
    p_ref[:, 2 * kw + vw + g_lo:2 * kw + vw + g_lo + MIX_V] = _proj(hn, wt_ref, g_row0 + g_lo, g_row0 + g_lo + MIX_V)


def _ret_kernel(hn0_ref, hnext_ref, wt_ref, cos_ref, sin_ref, dmat_ref, qdec_ref, kdec_ref, cdec_ref,
                ng_ref, o_ref, state_ref, p_ref, *, row0, nt):
    step = pl.program_id(0)
    kw, vw = MIX_HEADS * MIX_QK, MIX_HEADS * MIX_V
    g_row0 = row0 + 2 * kw + vw

    @pl.when(step == 0)
    def _():
        for h in range(MIX_HEADS):
            _project_head(hn0_ref[...], wt_ref, p_ref, row0, g_row0, h)

    @pl.when(step % nt == 0)
    def _():
        state_ref[...] = jnp.zeros(state_ref.shape, F32)

    hn_next = hnext_ref[...]
    cos, sin = cos_ref[...], sin_ref[...]
    scale = MIX_QK ** -0.5
    for h in range(MIX_HEADS):
        vl = slice(h * MIX_V, (h + 1) * MIX_V)
        q = _rotary(p_ref[:, h * MIX_QK:(h + 1) * MIX_QK], cos, sin)
        k = _rotary(p_ref[:, kw + h * MIX_QK:kw + (h + 1) * MIX_QK], cos, sin) * scale
        v16 = p_ref[:, 2 * kw + h * MIX_V:2 * kw + (h + 1) * MIX_V].astype(BF16)
        gate = _silu(p_ref[:, 2 * kw + vw + h * MIX_V:2 * kw + vw + (h + 1) * MIX_V])
        q16 = q.astype(BF16)
        inner = _dot_nt(q16, k.astype(BF16)) * dmat_ref[h]
        state = state_ref[h]
        out = _dot(inner.astype(BF16), v16) + _dot_nt((q * qdec_ref[h]).astype(BF16), state.astype(BF16))
        state_ref[h] = state * cdec_ref[h] + _dot_tn(v16, (k * kdec_ref[h]).astype(BF16))
        _project_head(hn_next, wt_ref, p_ref, row0, g_row0, h)
        mu = jnp.mean(out, axis=-1, keepdims=True)
        cen = out - mu
        var = jnp.mean(cen * cen, axis=-1, keepdims=True)
        o = cen * lax.rsqrt(var + NORM_EPS) * ng_ref[:, vl]
        o_ref[:, vl] = (o * gate).astype(o_ref.dtype)


def _rope_tables(seqlen):
    half = MIX_QK // 2
    inv_freq = (np.float32(ROPE_BASE) ** (-np.arange(half, dtype=np.float32) / half)).astype(np.float32)
    ang = (np.arange(seqlen, dtype=np.float32)[:, None] * inv_freq[None, :]).astype(np.float64)
    cos, sin = np.cos(ang), np.sin(ang)
    return (jnp.asarray(np.concatenate([cos, cos], axis=1), F32),
            jnp.asarray(np.concatenate([-sin, sin], axis=1), F32))


def retention_mixer(hn, wt_in, row0, norm_g, batch, seqlen):
    c = MIX_TILE
    nt = seqlen // c
    d_model = hn.shape[1]
    kw, vw = MIX_HEADS * MIX_QK, MIX_HEADS * MIX_V
    assert seqlen % c == 0 and hn.shape[0] == batch * seqlen
    assert wt_in.shape[1] == d_model and row0 + 2 * kw + 2 * vw <= wt_in.shape[0]
    cos, sin = _rope_tables(seqlen)
    log_gamma = np.log(1.0 - 2.0 ** (-5.0 - np.arange(MIX_HEADS, dtype=np.float64)))
    pos = np.arange(c, dtype=np.float64)
    diff = pos[:, None] - pos[None, :]
    dmat = jnp.asarray(np.where(diff >= 0, np.exp(log_gamma[:, None, None] * np.maximum(diff, 0.0)), 0.0), F32)
    lanes = lambda col: jnp.asarray(np.broadcast_to(col[:, :, None], (MIX_HEADS, c, MIX_QK)), F32)
    qdec = lanes(np.exp(log_gamma[:, None] * (pos + 1.0)))
    kdec = lanes(np.exp(log_gamma[:, None] * (c - 1.0 - pos)))
    cdec = jnp.asarray(np.broadcast_to(np.exp(log_gamma * c)[:, None, None], (MIX_HEADS, 1, MIX_QK)), F32)
    consts = [dmat, qdec, kdec, cdec, norm_g.astype(F32).reshape(1, vw)]
    ntiles = batch * nt
    rope = pl.BlockSpec((c, MIX_QK), lambda s: (s % nt, 0))
    return pl.pallas_call(
        functools.partial(_ret_kernel, row0=row0, nt=nt), grid=(ntiles,),
        in_specs=[pl.BlockSpec((c, d_model), lambda s: (0, 0)),
                  pl.BlockSpec((c, d_model), lambda s: (jnp.minimum(s + 1, ntiles - 1), 0)),
                  _resident(wt_in.shape), rope, rope] + [_resident(x.shape) for x in consts],
        out_specs=pl.BlockSpec((c, vw), lambda s: (s, 0)),
        out_shape=jax.ShapeDtypeStruct((batch * seqlen, vw), BF16),
        scratch_shapes=[pltpu.VMEM((MIX_HEADS, MIX_V, MIX_QK), F32), pltpu.VMEM((c, 2 * kw + 2 * vw), F32)],
        compiler_params=_params("arbitrary"), name="retention_mixer",
    )(hn, hn, wt_in, cos, sin, *consts)


GLA_CHUNK = 64
GLA_TAU = 16.0


def _log_sigmoid(x):
    return jnp.minimum(x, 0.0) - jnp.log1p(jnp.exp(-jnp.abs(x)))


def _gla_kernel(hn0_ref, hnext_ref, wt_ref, gw_ref, gb_ref, ng_ref, cumsel_ref,
                o_ref, state_ref, p_ref, *, row0, rank, nt):
    tl = hnext_ref.shape[0]
    nchunk = tl // GLA_CHUNK
    kw, vw = MIX_HEADS * MIX_QK, MIX_HEADS * MIX_V
    step = pl.program_id(0)
    a_row0 = row0 + 2 * kw + vw
    g_row0 = a_row0 + rank
    a_col0 = 2 * kw + 2 * vw

    @pl.when(step == 0)
    def _():
        hn0 = hn0_ref[...]
        p_ref[:, a_col0:] = _proj_narrow(hn0, wt_ref, a_row0, a_row0 + rank)
        for h in range(MIX_HEADS):
            _project_head(hn0, wt_ref, p_ref, row0, g_row0, h)

    @pl.when(step % nt == 0)
    def _():
        state_ref[...] = jnp.zeros(state_ref.shape, F32)

    hn_next = hnext_ref[...]
    a_lr = p_ref[:, a_col0:]
    log_a = _log_sigmoid(_dot(a_lr.astype(BF16), gw_ref[...]) + gb_ref[...]) / GLA_TAU
    p_ref[:, a_col0:] = _proj_narrow(hn_next, wt_ref, a_row0, a_row0 + rank)
    hi, mid, lo = _split3(log_a)
    zeros = jnp.zeros((GLA_CHUNK, kw), BF16)
    bcums, blasts = [], []
    for c in range(nchunk):
        rows = slice(c * GLA_CHUNK, (c + 1) * GLA_CHUNK)
        terms = jnp.concatenate([hi[rows, :], mid[rows, :], lo[rows, :], zeros], axis=0)
        bc = _dot(cumsel_ref[...], terms)
        bcums.append(bc)
        blasts.append(jnp.broadcast_to(bc[GLA_CHUNK - 1:GLA_CHUNK, :], (GLA_CHUNK, kw)))
    bcum = jnp.concatenate(bcums, axis=0)
    blast = jnp.concatenate(blasts, axis=0)
    grow = jnp.exp(bcum)
    row = lax.broadcasted_iota(jnp.int32, (tl, tl), 0)
    col = lax.broadcasted_iota(jnp.int32, (tl, tl), 1)
    shift = GLA_CHUNK.bit_length() - 1
    keep = (row >= col) & (jnp.right_shift(row, shift) == jnp.right_shift(col, shift))
    bcum_t = bcum.T
    scale = MIX_QK ** -0.5
    for h in range(MIX_HEADS):
        kl = slice(h * MIX_QK, (h + 1) * MIX_QK)
        vl = slice(h * MIX_V, (h + 1) * MIX_V)
        k = p_ref[:, kw + h * MIX_QK:kw + (h + 1) * MIX_QK]
        q_in = (p_ref[:, kl] * scale * grow[:, kl]).astype(BF16)
        k_in = (k * jnp.exp(-bcum[:, kl])).astype(BF16)
        k_out = (k * jnp.exp(blast[:, kl] - bcum[:, kl])).astype(BF16)
        v = p_ref[:, 2 * kw + h * MIX_V:2 * kw + (h + 1) * MIX_V]
        gate = _silu(p_ref[:, 2 * kw + vw + h * MIX_V:2 * kw + vw + (h + 1) * MIX_V])
        v16 = v.astype(BF16)
        att = jnp.where(keep, _dot_nt(q_in, k_in), 0.0)
        out = _dot(att.astype(BF16), v16)
        _project_head(hn_next, wt_ref, p_ref, row0, g_row0, h)
        chunks = [slice(c * GLA_CHUNK, (c + 1) * GLA_CHUNK) for c in range(nchunk)]
        kvs = [_dot_tn(k_out[rows, :], v16[rows, :]) for rows in chunks]
        decays = [jnp.exp(bcum_t[h * MIX_QK:(h + 1) * MIX_QK, rows.stop - 1:rows.stop]) for rows in chunks]
        state = state_ref[h]
        inter = []
        for rows, kv, decay in zip(chunks, kvs, decays):
            inter.append(_dot(q_in[rows, :], state.astype(BF16)))
            state = state * decay + kv
        state_ref[h] = state
        out = out + jnp.concatenate(inter, axis=0)
        o = out * lax.rsqrt(jnp.mean(out * out, axis=-1, keepdims=True) + NORM_EPS) * ng_ref[:, vl]
        o_ref[:, vl] = (o * gate).astype(o_ref.dtype)


def gla_mixer(hn, wt_in, row0, gate_w, gate_b, norm_g, batch, seqlen):
    tl = MIX_TILE
    nt = seqlen // tl
    d_model = hn.shape[1]
    kw, vw = MIX_HEADS * MIX_QK, MIX_HEADS * MIX_V
    assert seqlen % tl == 0 and tl % GLA_CHUNK == 0 and hn.shape[0] == batch * seqlen
    rank = gate_w.shape[0]
    assert wt_in.shape[1] == d_model and row0 + 2 * kw + 2 * vw + rank <= wt_in.shape[0]
    gw = jnp.pad(gate_w.astype(F32), ((0, LANES - rank), (0, 0))).astype(BF16)
    t = np.arange(GLA_CHUNK)
    tril = t[:, None] >= t[None, :]
    cumsel = jnp.asarray(np.concatenate([tril, tril, tril, np.zeros_like(tril)], axis=1), BF16)
    consts = [gw, gate_b.astype(F32).reshape(1, kw), norm_g.astype(F32).reshape(1, vw), cumsel]
    ntiles = batch * nt
    return pl.pallas_call(
        functools.partial(_gla_kernel, row0=row0, rank=rank, nt=nt), grid=(ntiles,),
        in_specs=[pl.BlockSpec((tl, d_model), lambda s: (0, 0)),
                  pl.BlockSpec((tl, d_model), lambda s: (jnp.minimum(s + 1, ntiles - 1), 0)),
                  _resident(wt_in.shape)] + [_resident(x.shape) for x in consts],
        out_specs=pl.BlockSpec((tl, vw), lambda s: (s, 0)),
        out_shape=jax.ShapeDtypeStruct((batch * seqlen, vw), BF16),
        scratch_shapes=[pltpu.VMEM((MIX_HEADS, MIX_QK, MIX_V), F32),
                        pltpu.VMEM((tl, 2 * kw + 2 * vw + LANES), F32)],
        compiler_params=_params("arbitrary"), name="gla_mixer",
    )(hn, hn, wt_in, *consts)


def _row_blocks(rows, max_blocks):
    return max(n for n in range(1, max_blocks + 1) if rows % n == 0 and (rows // n) % BF16_ROWS == 0)


def kernel(x, norm_mix_g, norm_ffn_g, final_norm_g, ev_in_w, s5_lam_re, s5_lam_im, s5_log_step, s5_b_re, s5_b_im, s5_c_re, s5_c_im, s5_d, s5_glu_w, s5_glu_b, ssd_conv_w, ssd_conv_b, ssd_dt_bias, ssd_a_log, ssd_d, ssd_norm_g, ev_out_w, od_in_w, ret_norm_g, gla_gate_w, gla_gate_b, gla_norm_g, od_out_w, ffn_gate_w, ffn_up_w, ffn_down_w):
    bsz, seqlen, d = x.shape
    rows = bsz * seqlen
    depth = norm_mix_g.shape[0]
    assert depth == 2 and ev_in_w.shape[0] == 1 and od_in_w.shape[0] == 1, "even layer then odd layer"
    h = x.reshape(rows, d)
    kw, vw = gla_gate_w.shape[2], ret_norm_g.shape[1]
    s5_w = s5_glu_w.shape[1]
    hidden = ffn_gate_w.shape[2]

    wt_even = jnp.swapaxes(ev_in_w[0], 0, 1).astype(BF16)
    s5_steps, ssd_steps = seqlen // S5_TILE, rows // SSD_TILE
    y_a, wg0, wu0, wd0, wo0 = s5_mixer(
        h.reshape(bsz, seqlen, d), norm_mix_g[0], wt_even, s5_lam_re[0], s5_lam_im[0], s5_log_step[0],
        s5_b_re[0], s5_b_im[0], s5_c_re[0], s5_c_im[0], s5_d[0].reshape(-1), s5_glu_w[0], s5_glu_b[0],
        riders=((ffn_gate_w, (0,), s5_steps), (ffn_up_w, (0,), s5_steps), (ffn_down_w, (0,), s5_steps),
                (ev_out_w, (0,), s5_steps)))
    wt_odd_f32 = jnp.swapaxes(od_in_w[0], 0, 1)
    y_b, wg1, wu1, wd1, wo1, wt_odd = ssd_mixer(
        h, norm_mix_g[0], wt_even, s5_w, ssd_conv_w[0], ssd_conv_b[0], ssd_dt_bias[0], ssd_a_log[0],
        ssd_d[0], ssd_norm_g[0], bsz, seqlen,
        riders=((ffn_gate_w, (1,), ssd_steps), (ffn_up_w, (1,), ssd_steps),
                (ffn_down_w, (1,), _row_blocks(hidden, ssd_steps)), (od_out_w, (0,), ssd_steps),
                (wt_odd_f32, (), _row_blocks(wt_odd_f32.shape[0], ssd_steps))))
    h, hn = mix_ffn(h, y_a.reshape(rows, s5_w), y_b, wo0, norm_ffn_g[0], wg0, wu0, wd0, norm_mix_g[1],
                    final_norm=False)

    y_a = retention_mixer(hn, wt_odd, 0, ret_norm_g[0], bsz, seqlen)
    y_b = gla_mixer(hn, wt_odd, 2 * kw + 2 * vw, gla_gate_w[0], gla_gate_b[0], gla_norm_g[0], bsz, seqlen)
    h = mix_ffn(h, y_a, y_b, wo1, norm_ffn_g[1], wg1, wu1, wd1, final_norm_g, final_norm=True)
    return h.reshape(bsz, seqlen, d)
```
